```python
import math
import jax, jax.numpy as jnp
from jax import lax
import numpy as np

D_MODEL = 1024
BATCH = 4
SEQ = 4096
DEPTH = 1

CHUNK = 64
Q_BLOCK = 128
EPS = 1e-6
D_FF = 2816
N_MOD = 9
MLA_HEADS = 8
MLA_Q_RANK = 768
MLA_KV_RANK = 256
MLA_NOPE = 128
MLA_ROPE = 64
MLA_V = 128
ROPE_THETA = 10000.0
GDN_HEADS = 8
GDN_DK = 128
GDN_DV = 128
CONV_W = 4
N_BRANCH = 2
IN_SPLITS = (MLA_Q_RANK, MLA_KV_RANK, MLA_ROPE, GDN_HEADS * GDN_DK, GDN_HEADS * GDN_DK, GDN_HEADS * GDN_DV, GDN_HEADS, GDN_HEADS, GDN_HEADS * GDN_DV, N_BRANCH * D_MODEL)
D_IN = sum(IN_SPLITS)

kernel_name = 'hybrid_mla_gdn_macaron_adaln_block'


def rmsnorm(x, g):
    xf = x.astype(jnp.float32)
    xf = xf * lax.rsqrt(jnp.mean(xf * xf, axis=-1, keepdims=True) + EPS)
    return (xf * g.astype(jnp.float32)).astype(x.dtype)


def modulate(x, g, shift, scale):
    return rmsnorm(x, g) * (1.0 + scale[:, None, :]) + shift[:, None, :]


def swiglu(h, w_gate, w_up, w_down):
    return (jax.nn.silu(h @ w_gate) * (h @ w_up)) @ w_down


def l2norm(t):
    return t * lax.rsqrt(jnp.sum(t * t, axis=-1, keepdims=True) + EPS)


def rope(x, cos, sin):
    xf = x.astype(jnp.float32)
    x1, x2 = jnp.split(xf, 2, axis=-1)
    return jnp.concatenate([x1 * cos - x2 * sin, x2 * cos + x1 * sin], axis=-1).astype(x.dtype)


def mla_branch(q_lat, kv_lat, k_pe, g_q, w_uq, g_kv, w_ukv, cos, sin):
    B, S, _ = q_lat.shape
    H = MLA_HEADS
    q = (rmsnorm(q_lat, g_q) @ w_uq).reshape(B, S, H, MLA_NOPE + MLA_ROPE)
    q_nope = q[..., :MLA_NOPE]
    q_pe = rope(q[..., MLA_NOPE:], cos[:, None, :], sin[:, None, :])
    kv = (rmsnorm(kv_lat, g_kv) @ w_ukv).reshape(B, S, H, MLA_NOPE + MLA_V)
    k_nope = kv[..., :MLA_NOPE]
    v = kv[..., MLA_NOPE:]
    k_pe = rope(k_pe, cos, sin)
    scale = (MLA_NOPE + MLA_ROPE) ** -0.5
    nb = S // Q_BLOCK
    qn_blk = q_nope.reshape(B, nb, Q_BLOCK, H, MLA_NOPE).transpose(1, 0, 2, 3, 4)
    qp_blk = q_pe.reshape(B, nb, Q_BLOCK, H, MLA_ROPE).transpose(1, 0, 2, 3, 4)
    k_chunk = jnp.arange(S) // CHUNK

    def attend(args):
        qn, qp, blk = args
        s = jnp.einsum('bqhd,bkhd->bhqk', qn, k_nope) + jnp.einsum('bqhr,bkr->bhqk', qp, k_pe)
        s = s.astype(jnp.float32) * scale
        q_chunk = (blk * Q_BLOCK + jnp.arange(Q_BLOCK)) // CHUNK
        mask = k_chunk[None, :] <= q_chunk[:, None]
        s = jnp.where(mask, s, -jnp.inf)
        p = jax.nn.softmax(s, axis=-1).astype(v.dtype)
        return jnp.einsum('bhqk,bkhd->bqhd', p, v)

    o = lax.map(attend, (qn_blk, qp_blk, jnp.arange(nb)))
    return o.transpose(1, 0, 2, 3, 4).reshape(B, S, H * MLA_V)


def causal_depthwise_conv(x, w):
    C = x.shape[-1]
    xp = jnp.pad(x, ((0, 0), (CONV_W - 1, 0), (0, 0)))
    return lax.conv_general_dilated(xp, w[:, None, :].astype(x.dtype), window_strides=(1,), padding='VALID', dimension_numbers=('NWC', 'WIO', 'NWC'), feature_group_count=C)


def gated_deltanet_branch(qkv, a, b, z, w_conv, a_log, dt_bias, g_out):
    B, S, _ = qkv.shape
    N = S // CHUNK
    H, DK, DV = GDN_HEADS, GDN_DK, GDN_DV
    f32 = jnp.float32
    qkv = jax.nn.silu(causal_depthwise_conv(qkv, w_conv))
    q, k, v = jnp.split(qkv, [H * DK, 2 * H * DK], axis=-1)

    def to_chunks(t, d):
        return t.reshape(B, N, CHUNK, H, d).transpose(0, 3, 1, 2, 4).astype(f32)

    q = l2norm(to_chunks(q, DK)) * (DK ** -0.5)
    k = l2norm(to_chunks(k, DK))
    v = to_chunks(v, DV)
    beta = jax.nn.sigmoid(b.astype(f32)).reshape(B, N, CHUNK, H).transpose(0, 3, 1, 2)
    g = -jnp.exp(a_log.astype(f32)) * jax.nn.softplus(a.astype(f32) + dt_bias.astype(f32))
    g = g.reshape(B, N, CHUNK, H).transpose(0, 3, 1, 2)
    decay = jnp.cumsum(g, axis=-1)
    idx = jnp.arange(CHUNK)
    causal = idx[:, None] >= idx[None, :]
    strict = idx[:, None] > idx[None, :]
    L = jnp.exp(jnp.where(causal, decay[..., :, None] - decay[..., None, :], -jnp.inf))
    k_beta = k * beta[..., None]
    A = jnp.where(strict, jnp.einsum('bhnid,bhnjd->bhnij', k_beta, k) * L, 0.0)
    eye = jnp.eye(CHUNK, dtype=f32)
    T = lax.linalg.triangular_solve(A + eye, jnp.broadcast_to(eye, A.shape), left_side=True, lower=True, unit_diagonal=True)
    w = jnp.einsum('bhnij,bhnjd->bhnid', T, k_beta * jnp.exp(decay)[..., None])
    u = jnp.einsum('bhnij,bhnje->bhnie', T, v * beta[..., None])
    attn_intra = jnp.einsum('bhnid,bhnjd->bhnij', q, k) * L
    q_dec = q * jnp.exp(decay)[..., None]
    k_dec = k * jnp.exp(decay[..., -1:] - decay)[..., None]
    chunk_decay = jnp.exp(decay[..., -1])
    xs = tuple(jnp.moveaxis(t, 2, 0) for t in (q_dec, k_dec, w, u, attn_intra, chunk_decay))

    def step(state, inp):
        qd, kd, w_n, u_n, a_n, cd = inp
        v_new = u_n - jnp.einsum('bhcd,bhde->bhce', w_n, state)
        o = jnp.einsum('bhcd,bhde->bhce', qd, state) + jnp.einsum('bhij,bhje->bhie', a_n, v_new)
        state = state * cd[..., None, None] + jnp.einsum('bhcd,bhce->bhde', kd, v_new)
        return state, o

    state0 = jnp.zeros((B, H, DK, DV), f32)
    _, o = lax.scan(step, state0, xs)
    o = o.transpose(1, 0, 3, 2, 4).reshape(B, S, H, DV)
    o = rmsnorm(o, g_out) * jax.nn.silu(z.reshape(B, S, H, DV).astype(f32))
    return o.reshape(B, S, H * DV).astype(z.dtype)


def setup_inputs(seed: int = 0) -> dict:
    key = jax.random.key(seed)
    ks = jax.random.split(key, 32)
    f32 = jnp.float32
    L = DEPTH

    def dense(k, shape, fan_in, mult=1.0):
        return jax.random.normal(k, shape, f32) * (mult * fan_in ** -0.5)

    def gain(k, shape):
        return 1.0 + 0.05 * jax.random.normal(k, shape, f32)

    dt = jnp.exp(jax.random.uniform(ks[17], (L, GDN_HEADS), f32, math.log(1e-3), math.log(1e-1)))
    return {
        'x': jax.random.normal(ks[0], (BATCH, SEQ, D_MODEL), f32),
        'c': jax.random.normal(ks[1], (BATCH, D_MODEL), f32),
        'w_ada': dense(ks[2], (L, D_MODEL, N_MOD * D_MODEL), D_MODEL, 0.5),
        'b_ada': 0.01 * jax.random.normal(ks[3], (L, N_MOD * D_MODEL), f32),
        'g_ffn1': gain(ks[4], (L, D_MODEL)),
        'w1_gate': dense(ks[5], (L, D_MODEL, D_FF), D_MODEL),
        'w1_up': dense(ks[6], (L, D_MODEL, D_FF), D_MODEL),
        'w1_down': dense(ks[7], (L, D_FF, D_MODEL), D_FF),
        'g_mix': gain(ks[8], (L, D_MODEL)),
        'w_in': dense(ks[9], (L, D_MODEL, D_IN), D_MODEL),
        'g_q_lat': gain(ks[10], (L, MLA_Q_RANK)),
        'w_uq': dense(ks[11], (L, MLA_Q_RANK, MLA_HEADS * (MLA_NOPE + MLA_ROPE)), MLA_Q_RANK),
        'g_kv_lat': gain(ks[12], (L, MLA_KV_RANK)),
        'w_ukv': dense(ks[13], (L, MLA_KV_RANK, MLA_HEADS * (MLA_NOPE + MLA_V)), MLA_KV_RANK),
        'w_conv': dense(ks[14], (L, CONV_W, 3 * GDN_HEADS * GDN_DK), CONV_W),
        'a_log': jnp.log(jax.random.uniform(ks[15], (L, GDN_HEADS), f32, 1.0, 16.0)),
        'dt_bias': dt + jnp.log(-jnp.expm1(-dt)),
        'g_gdn_out': gain(ks[16], (L, GDN_DV)),
        'w_o_mla': dense(ks[18], (L, MLA_HEADS * MLA_V, D_MODEL), MLA_HEADS * MLA_V),
        'w_o_gdn': dense(ks[19], (L, GDN_HEADS * GDN_DV, D_MODEL), GDN_HEADS * GDN_DV),
        'w_out': dense(ks[20], (L, D_MODEL, D_MODEL), D_MODEL),
        'g_ffn2': gain(ks[21], (L, D_MODEL)),
        'w2_gate': dense(ks[22], (L, D_MODEL, D_FF), D_MODEL),
        'w2_up': dense(ks[23], (L, D_MODEL, D_FF), D_MODEL),
        'w2_down': dense(ks[24], (L, D_FF, D_MODEL), D_FF),
        'g_final': gain(ks[25], (D_MODEL,)),
    }


def reference(x, c, w_ada, b_ada, g_ffn1, w1_gate, w1_up, w1_down, g_mix, w_in, g_q_lat, w_uq, g_kv_lat, w_ukv, w_conv, a_log, dt_bias, g_gdn_out, w_o_mla, w_o_gdn, w_out, g_ffn2, w2_gate, w2_up, w2_down, g_final):
    S = x.shape[1]
    pos = jnp.arange(S, dtype=jnp.float32)
    inv_freq = ROPE_THETA ** (-jnp.arange(0, MLA_ROPE, 2, dtype=jnp.float32) / MLA_ROPE)
    ang = pos[:, None] * inv_freq[None, :]
    cos, sin = jnp.cos(ang), jnp.sin(ang)
    split_at = np.cumsum(IN_SPLITS)[:-1].tolist()
    c_act = jax.nn.silu(c)
    for l in range(DEPTH):
        ada = c_act @ w_ada[l] + b_ada[l]
        sh1, sc1, gt1, sh2, sc2, gt2, sh3, sc3, gt3 = jnp.split(ada, N_MOD, axis=-1)
        h = modulate(x, g_ffn1[l], sh1, sc1)
        x = x + 0.5 * gt1[:, None, :] * swiglu(h, w1_gate[l], w1_up[l], w1_down[l])
        h = modulate(x, g_mix[l], sh2, sc2)
        proj = h @ w_in[l]
        q_lat, kv_lat, k_pe, gq, gk, gv, ga, gb, gz, gates = jnp.split(proj, split_at, axis=-1)
        y_mla = mla_branch(q_lat, kv_lat, k_pe, g_q_lat[l], w_uq[l], g_kv_lat[l], w_ukv[l], cos, sin) @ w_o_mla[l]
        y_gdn = gated_deltanet_branch(jnp.concatenate([gq, gk, gv], axis=-1), ga, gb, gz, w_conv[l], a_log[l], dt_bias[l], g_gdn_out[l]) @ w_o_gdn[l]
        gate_mla, gate_gdn = jnp.split(jax.nn.sigmoid(gates), N_BRANCH, axis=-1)
        mixed = (gate_mla * y_mla + gate_gdn * y_gdn) @ w_out[l]
        x = x + gt2[:, None, :] * mixed
        h = modulate(x, g_ffn2[l], sh3, sc3)
        x = x + 0.5 * gt3[:, None, :] * swiglu(h, w2_gate[l], w2_up[l], w2_down[l])
    return rmsnorm(x, g_final)
```

```python
import functools

import numpy as np
import jax
import jax.numpy as jnp
from jax import lax
from jax.experimental import pallas as pl
from jax.experimental.pallas import tpu as pltpu

F32 = jnp.float32
BF16 = jnp.bfloat16

EPS = 1e-6
D_MODEL = 1024
D_FF = 2816
N_MOD = 9
CHUNK = 64
MLA_HEADS = 8
MLA_Q_RANK = 768
MLA_KV_RANK = 256
MLA_NOPE = 128
MLA_ROPE = 64
MLA_V = 128
ROPE_THETA = 10000.0
GDN_HEADS = 8
GDN_DK = 128
GDN_DV = 128
CONV_W = 4
LANE = 128
QK_PAD = 256
NEG_BIG = -1e30
VMEM_LIMIT = 56 * 1024 * 1024


def _cparams(sem):
    return pltpu.CompilerParams(dimension_semantics=sem, vmem_limit_bytes=VMEM_LIMIT)


def _sigmoid(x):
    return 1.0 / (1.0 + jnp.exp(-x))


def _silu(x):
    return x * _sigmoid(x)


def _rms(x):
    return x * lax.rsqrt(jnp.mean(x * x, axis=-1, keepdims=True) + EPS)


def _modulate(x, g, shift, scale):
    return (_rms(x) * g) * (1.0 + scale) + shift


def _ada_kernel(c_ref, w_ref, b_ref, o_ref):
    c = _silu(c_ref[...]).astype(BF16)
    o_ref[...] = jnp.dot(c, w_ref[...].astype(BF16), preferred_element_type=F32) + b_ref[...]


def _ada(c_pad, w_ada, b_ada):
    m, d = c_pad.shape
    n = w_ada.shape[1]
    tn = 1024
    return pl.pallas_call(
        _ada_kernel,
        grid=(n // tn,),
        in_specs=[pl.BlockSpec((m, d), lambda j: (0, 0)),
                  pl.BlockSpec((d, tn), lambda j: (0, j)),
                  pl.BlockSpec((1, tn), lambda j: (0, j))],
        out_specs=pl.BlockSpec((m, tn), lambda j: (0, j)),
        out_shape=jax.ShapeDtypeStruct((m, n), F32),
        compiler_params=_cparams(("parallel",)),
        name="ada",
    )(c_pad, w_ada, b_ada)


def _ffn_kernel(x_ref, ada_ref, g_ref, wg_ref, wu_ref, wd_ref, gf_ref, o_ref, h_ref,
                *, mod, nf, final_norm):
    f = pl.program_id(1)

    @pl.when(f == 0)
    def _():
        h = _modulate(x_ref[...], g_ref[...], ada_ref[3 * mod:3 * mod + 1, :],
                      ada_ref[3 * mod + 1:3 * mod + 2, :])
        h_ref[...] = h.astype(BF16)
        o_ref[...] = jnp.zeros_like(o_ref)

    h = h_ref[...]
    a = jnp.dot(h, wg_ref[...], preferred_element_type=F32)
    b = jnp.dot(h, wu_ref[...], preferred_element_type=F32)
    act = (_silu(a) * b).astype(BF16)
    o_ref[...] += jnp.dot(act, wd_ref[...], preferred_element_type=F32)

    @pl.when(f == nf - 1)
    def _():
        gt = ada_ref[3 * mod + 2:3 * mod + 3, :]
        y = x_ref[...] + (0.5 * gt) * o_ref[...]
        if final_norm:
            y = _rms(y) * gf_ref[...]
        o_ref[...] = y


def _ffn(x, ada3, g, wg, wu, wd, gf, *, mod, seq, final_norm, tm=1024, tf=256):
    t, d = x.shape
    nf = D_FF // tf
    tiles_per_batch = seq // tm
    kern = functools.partial(_ffn_kernel, mod=mod, nf=nf, final_norm=final_norm)
    return pl.pallas_call(
        kern,
        grid=(t // tm, nf),
        in_specs=[pl.BlockSpec((tm, d), lambda i, f: (i, 0)),
                  pl.BlockSpec((None, N_MOD, d), lambda i, f: (i // tiles_per_batch, 0, 0)),
                  pl.BlockSpec((1, d), lambda i, f: (0, 0)),
                  pl.BlockSpec((d, tf), lambda i, f: (0, f)),
                  pl.BlockSpec((d, tf), lambda i, f: (0, f)),
                  pl.BlockSpec((tf, d), lambda i, f: (f, 0)),
                  pl.BlockSpec((1, d), lambda i, f: (0, 0))],
        out_specs=pl.BlockSpec((tm, d), lambda i, f: (i, 0)),
        out_shape=jax.ShapeDtypeStruct((t, d), F32),
        scratch_shapes=[pltpu.VMEM((tm, d), BF16)],
        compiler_params=_cparams(("parallel", "arbitrary")),
        name="ffn%d" % mod,
    )(x, ada3, g, wg, wu, wd, gf)


def _modmm_kernel(x_ref, ada_ref, g_ref, w_ref, o_ref, h_ref, *, mod):
    @pl.when(pl.program_id(1) == 0)
    def _():
        h = _modulate(x_ref[...], g_ref[...], ada_ref[3 * mod:3 * mod + 1, :],
                      ada_ref[3 * mod + 1:3 * mod + 2, :])
        h_ref[...] = h.astype(BF16)

    o_ref[...] = jnp.dot(h_ref[...], w_ref[...], preferred_element_type=F32).astype(o_ref.dtype)


def _modmm(x, ada3, g, w, *, mod, seq, out_dtype, tm, tn, name):
    t, d = x.shape
    n = w.shape[1]
    tiles_per_batch = seq // tm
    return pl.pallas_call(
        functools.partial(_modmm_kernel, mod=mod),
        grid=(t // tm, n // tn),
        in_specs=[pl.BlockSpec((tm, d), lambda i, j: (i, 0)),
                  pl.BlockSpec((None, N_MOD, d), lambda i, j: (i // tiles_per_batch, 0, 0)),
                  pl.BlockSpec((1, d), lambda i, j: (0, 0)),
                  pl.BlockSpec((d, tn), lambda i, j: (0, j))],
        out_specs=pl.BlockSpec((tm, tn), lambda i, j: (i, j)),
        out_shape=jax.ShapeDtypeStruct((t, n), out_dtype),
        scratch_shapes=[pltpu.VMEM((tm, d), BF16)],
        compiler_params=_cparams(("parallel", "arbitrary")),
        name=name,
    )(x, ada3, g, w)


def _mla_prep_kernel(lat_ref, gq_ref, gkv_ref, wuq_ref, wukv_ref, cos_ref, sin_ref,
                     q_ref, k_ref, v_ref, *, scale):
    lat = lat_ref[...]
    qn = (_rms(lat[:, :MLA_Q_RANK]) * gq_ref[...]).astype(BF16)
    kvn = (_rms(lat[:, MLA_Q_RANK:MLA_Q_RANK + MLA_KV_RANK]) * gkv_ref[...]).astype(BF16)
    kpe = lat[:, MLA_Q_RANK + MLA_KV_RANK:MLA_Q_RANK + MLA_KV_RANK + LANE]
    q = jnp.dot(qn, wuq_ref[...], preferred_element_type=F32)
    kv = jnp.dot(kvn, wukv_ref[...], preferred_element_type=F32)
    cos = cos_ref[...]
    sin = sin_ref[...]
    kpe_r = (kpe * cos + pltpu.roll(kpe, LANE // 2, axis=1) * sin).astype(BF16)
    lane = lax.broadcasted_iota(jnp.int32, (1, LANE), 1)
    nope_w = MLA_HEADS * MLA_NOPE
    for p in range(MLA_HEADS // 2):
        pe = q[:, nope_w + LANE * p:nope_w + LANE * (p + 1)]
        pe_r = (pe * cos + pltpu.roll(pe, LANE // 2, axis=1) * sin) * scale
        for e in range(2):
            h = 2 * p + e
            keep = ((lane // 32) % 2) == e
            q_ref[h, :, 0:LANE] = (q[:, LANE * h:LANE * (h + 1)] * scale).astype(BF16)
            q_ref[h, :, LANE:2 * LANE] = jnp.where(keep, pe_r, 0.0).astype(BF16)
            k_ref[h, :, 0:LANE] = kv[:, LANE * h:LANE * (h + 1)].astype(BF16)
            k_ref[h, :, LANE:2 * LANE] = kpe_r
    v_ref[...] = kv[:, nope_w:].astype(BF16)


def _mla_prep(lat, gq, gkv, wuq, wukv, cos_a, sin_a, *, batch, seq, tm=512):
    t, nl = lat.shape
    tiles_per_batch = seq // tm
    scale = float((MLA_NOPE + MLA_ROPE) ** -0.5)
    qk_shape = jax.ShapeDtypeStruct((batch, MLA_HEADS, seq, QK_PAD), BF16)
    qk_spec = pl.BlockSpec((None, MLA_HEADS, tm, QK_PAD),
                           lambda i: (i // tiles_per_batch, 0, i % tiles_per_batch, 0))
    return pl.pallas_call(
        functools.partial(_mla_prep_kernel, scale=scale),
        grid=(t // tm,),
        in_specs=[pl.BlockSpec((tm, nl), lambda i: (i, 0)),
                  pl.BlockSpec((1, MLA_Q_RANK), lambda i: (0, 0)),
                  pl.BlockSpec((1, MLA_KV_RANK), lambda i: (0, 0)),
                  pl.BlockSpec(wuq.shape, lambda i: (0, 0)),
                  pl.BlockSpec(wukv.shape, lambda i: (0, 0)),
                  pl.BlockSpec((tm, LANE), lambda i: (i % tiles_per_batch, 0)),
                  pl.BlockSpec((tm, LANE), lambda i: (i % tiles_per_batch, 0))],
        out_specs=[qk_spec, qk_spec,
                   pl.BlockSpec((tm, MLA_HEADS * MLA_V), lambda i: (i, 0))],
        out_shape=[qk_shape, qk_shape,
                   jax.ShapeDtypeStruct((t, MLA_HEADS * MLA_V), BF16)],
        compiler_params=_cparams(("parallel",)),
        name="mla_prep",
    )(lat, gq, gkv, wuq, wukv, cos_a, sin_a)


def _attn_kernel(q_ref, k_ref, v_ref, o_ref, *, tq, tk):
    qi = pl.program_id(2)
    q = q_ref[...]

    def step(kv_i, carry, masked):
        m, l, acc = carry
        start = pl.multiple_of(kv_i * tk, tk)
        k = k_ref[pl.ds(start, tk), :]
        v = v_ref[pl.ds(start, tk), :]
        s = lax.dot_general(q, k, (((1,), (1,)), ((), ())), preferred_element_type=F32)
        if masked:
            row = lax.broadcasted_iota(jnp.int32, (tq, tk), 0) // CHUNK
            col = lax.broadcasted_iota(jnp.int32, (tq, tk), 1) // CHUNK
            s = jnp.where(col <= row, s, NEG_BIG)
        m_new = jnp.maximum(m, jnp.max(s, axis=-1, keepdims=True))
        p = jnp.exp(s - m_new)
        alpha = jnp.exp(m - m_new)
        l = alpha * l + jnp.sum(p, axis=-1, keepdims=True)
        acc = alpha * acc + jnp.dot(p.astype(BF16), v, preferred_element_type=F32)
        return m_new, l, acc

    init = (jnp.full((tq, 1), NEG_BIG, F32), jnp.zeros((tq, 1), F32), jnp.zeros((tq, MLA_V), F32))
    carry = lax.fori_loop(0, qi, lambda i, c: step(i, c, False), init)
    m, l, acc = step(qi, carry, True)
    o_ref[...] = (acc / l).astype(o_ref.dtype)


def _attention(q, k, v, *, batch, seq, tq=512):
    nq = seq // tq
    return pl.pallas_call(
        functools.partial(_attn_kernel, tq=tq, tk=tq),
        grid=(batch, MLA_HEADS, nq),
        in_specs=[pl.BlockSpec((None, None, tq, QK_PAD), lambda b, h, i: (b, h, i, 0)),
                  pl.BlockSpec((None, None, seq, QK_PAD), lambda b, h, i: (b, h, 0, 0)),
                  pl.BlockSpec((seq, MLA_V), lambda b, h, i: (b, h))],
        out_specs=pl.BlockSpec((tq, MLA_V), lambda b, h, i: (b * nq + i, h)),
        out_shape=jax.ShapeDtypeStruct((batch * seq, MLA_HEADS * MLA_V), BF16),
        compiler_params=_cparams(("parallel", "parallel", "arbitrary")),
        name="attention",
    )(q, k, v)


def _split3(x):
    hi = x.astype(BF16)
    r = x - hi.astype(F32)
    mid = r.astype(BF16)
    lo = (r - mid.astype(F32)).astype(BF16)
    return hi, mid, lo


def _gdn_kernel(qkv_ref, z_ref, ab_ref, wconv_ref, prm_ref, o_ref, xbuf, ybuf, state):
    n = pl.program_id(1)
    c = CHUNK
    hw = GDN_HEADS * GDN_DK

    @pl.when(n == 0)
    def _():
        xbuf[0:8, :] = jnp.zeros((8, xbuf.shape[1]), F32)
        state[...] = jnp.zeros_like(state)

    xbuf[8:8 + c, :] = qkv_ref[...].astype(F32)
    w = wconv_ref[...]
    y = w[3:4, :] * xbuf[8:8 + c, :]
    for j in range(1, CONV_W):
        y = y + w[3 - j:4 - j, :] * xbuf[8 - j:8 - j + c, :]
    ybuf[...] = _silu(y)
    xbuf[0:8, :] = xbuf[c:c + 8, :]

    ab = ab_ref[...]
    a_neg = -jnp.exp(prm_ref[0:1, :])
    xg = ab + prm_ref[1:2, :]
    softplus = jnp.maximum(xg, 0.0) + jnp.log1p(jnp.exp(-jnp.abs(xg)))
    g = a_neg * softplus
    beta_all = _sigmoid(ab)
    ri = lax.broadcasted_iota(jnp.int32, (c, c), 0)
    ci = lax.broadcasted_iota(jnp.int32, (c, c), 1)
    causal = ri >= ci
    strict = ri > ci
    tri = jnp.where(causal, 1.0, 0.0).astype(BF16)
    g_hi, g_mid, g_lo = _split3(g)
    dcol = (jnp.dot(tri, g_hi, preferred_element_type=F32)
            + jnp.dot(tri, g_mid, preferred_element_type=F32)
            + jnp.dot(tri, g_lo, preferred_element_type=F32))
    drow = dcol.T
    dlast = dcol[c - 1:c, :]
    e_all = jnp.exp(dcol)
    kdf_all = jnp.exp(dlast - dcol)
    cd_all = jnp.exp(dlast)
    eye = jnp.where(ri == ci, 1.0, 0.0)
    g_out = prm_ref[2:3, :]
    nt = (((1,), (1,)), ((), ()))

    for h in range(GDN_HEADS):
        sl = slice(h * GDN_DK, (h + 1) * GDN_DK)
        qh = ybuf[:, h * GDN_DK:(h + 1) * GDN_DK]
        kh = ybuf[:, hw + h * GDN_DK:hw + (h + 1) * GDN_DK]
        vh = ybuf[:, 2 * hw + h * GDN_DV:2 * hw + (h + 1) * GDN_DV]
        qh = qh * lax.rsqrt(jnp.sum(qh * qh, axis=-1, keepdims=True) + EPS) * (GDN_DK ** -0.5)
        kh = kh * lax.rsqrt(jnp.sum(kh * kh, axis=-1, keepdims=True) + EPS)
        beta = beta_all[:, GDN_HEADS + h:GDN_HEADS + h + 1]
        di = dcol[:, h:h + 1]
        dj = drow[h:h + 1, :]
        ed = e_all[:, h:h + 1]
        kdf = kdf_all[:, h:h + 1]
        cd = cd_all[:, h:h + 1]
        lmat = jnp.exp(jnp.where(causal, di - dj, NEG_BIG))
        kb = kh * beta
        kh16 = kh.astype(BF16)
        kk = lax.dot_general(kb.astype(BF16), kh16, nt, preferred_element_type=F32)
        qk = lax.dot_general(qh.astype(BF16), kh16, nt, preferred_element_type=F32)
        a_mat = jnp.where(strict, kk * lmat, 0.0)
        attn = qk * lmat
        pw = a_mat
        t_inv = eye - a_mat
        for _ in range(5):
            pw16 = pw.astype(BF16)
            pw = jnp.dot(pw16, pw16, preferred_element_type=F32)
            t_inv = t_inv + jnp.dot(t_inv.astype(BF16), pw.astype(BF16), preferred_element_type=F32)
        rhs = jnp.concatenate([kb * ed, vh * beta], axis=1).astype(BF16)
        wu = jnp.dot(t_inv.astype(BF16), rhs, preferred_element_type=F32)
        w_n = wu[:, :GDN_DK]
        u_n = wu[:, GDN_DK:]
        s_old = state[h]
        lhs = jnp.concatenate([w_n, qh * ed], axis=0).astype(BF16)
        ws_qs = jnp.dot(lhs, s_old.astype(BF16), preferred_element_type=F32)
        v_new = u_n - ws_qs[:c, :]
        v16 = v_new.astype(BF16)
        o = ws_qs[c:, :] + jnp.dot(attn.astype(BF16), v16, preferred_element_type=F32)
        kd_t = (kh * kdf).T.astype(BF16)
        state[h] = s_old * cd + jnp.dot(kd_t, v16, preferred_element_type=F32)
        zh = z_ref[:, sl].astype(F32)
        o_ref[:, sl] = ((_rms(o) * g_out) * _silu(zh)).astype(o_ref.dtype)


def _gdn(pa, lat, wconv, prm, *, batch, seq):
    nchunk = seq // CHUNK
    hw3 = 3 * GDN_HEADS * GDN_DK
    hv = GDN_HEADS * GDN_DV
    ab_blk = (MLA_Q_RANK + MLA_KV_RANK + LANE) // LANE
    z_blk = pa.shape[1] // hv - 1
    return pl.pallas_call(
        _gdn_kernel,
        grid=(batch, nchunk),
        in_specs=[pl.BlockSpec((CHUNK, hw3), lambda b, n: (b * nchunk + n, 0)),
                  pl.BlockSpec((CHUNK, hv), lambda b, n: (b * nchunk + n, z_blk)),
                  pl.BlockSpec((CHUNK, LANE), lambda b, n: (b * nchunk + n, ab_blk)),
                  pl.BlockSpec(wconv.shape, lambda b, n: (0, 0)),
                  pl.BlockSpec(prm.shape, lambda b, n: (0, 0))],
        out_specs=pl.BlockSpec((CHUNK, hv), lambda b, n: (b * nchunk + n, 0)),
        out_shape=jax.ShapeDtypeStruct((batch * seq, hv), BF16),
        scratch_shapes=[pltpu.VMEM((CHUNK + 8, hw3), F32),
                        pltpu.VMEM((CHUNK, hw3), F32),
                        pltpu.VMEM((GDN_HEADS, GDN_DK, GDN_DV), F32)],
        compiler_params=_cparams(("parallel", "arbitrary")),
        name="gdn",
    )(pa, pa, lat, wconv, prm)


def _merge_kernel(om_ref, og_ref, gm_ref, gg_ref, x_ref, ada_ref, wm_ref, wg_ref, wo_ref, o_ref):
    ym = jnp.dot(om_ref[...], wm_ref[...], preferred_element_type=F32)
    yg = jnp.dot(og_ref[...], wg_ref[...], preferred_element_type=F32)
    mix = _sigmoid(gm_ref[...].astype(F32)) * ym + _sigmoid(gg_ref[...].astype(F32)) * yg
    mixed = jnp.dot(mix.astype(BF16), wo_ref[...], preferred_element_type=F32)
    o_ref[...] = x_ref[...] + ada_ref[5:6, :] * mixed


def _merge(o_mla, o_gdn, pa, x1, ada3, wm, wg, wo, *, seq, tm=512):
    t, d = x1.shape
    tiles_per_batch = seq // tm
    gate_blk = 3 * GDN_HEADS * GDN_DK // d
    wspec = pl.BlockSpec((d, d), lambda i: (0, 0))
    return pl.pallas_call(
        _merge_kernel,
        grid=(t // tm,),
        in_specs=[pl.BlockSpec((tm, d), lambda i: (i, 0)),
                  pl.BlockSpec((tm, d), lambda i: (i, 0)),
                  pl.BlockSpec((tm, d), lambda i: (i, gate_blk)),
                  pl.BlockSpec((tm, d), lambda i: (i, gate_blk + 1)),
                  pl.BlockSpec((tm, d), lambda i: (i, 0)),
                  pl.BlockSpec((None, N_MOD, d), lambda i: (i // tiles_per_batch, 0, 0)),
                  wspec, wspec, wspec],
        out_specs=pl.BlockSpec((tm, d), lambda i: (i, 0)),
        out_shape=jax.ShapeDtypeStruct((t, d), F32),
        compiler_params=_cparams(("parallel",)),
        name="merge",
    )(o_mla, o_gdn, pa, pa, x1, ada3, wm, wg, wo)


def _uq_perm():
    hd = MLA_NOPE + MLA_ROPE
    half = MLA_ROPE // 2
    cols = [h * hd + d for h in range(MLA_HEADS) for d in range(MLA_NOPE)]
    for p in range(MLA_HEADS // 2):
        for part in range(2):
            for e in range(2):
                h = 2 * p + e
                cols += [h * hd + MLA_NOPE + part * half + r for r in range(half)]
    return np.asarray(cols, np.int32)


def _ukv_perm():
    hd = MLA_NOPE + MLA_V
    k = [h * hd + d for h in range(MLA_HEADS) for d in range(MLA_NOPE)]
    v = [h * hd + MLA_NOPE + d for h in range(MLA_HEADS) for d in range(MLA_V)]
    return np.asarray(k + v, np.int32)


def kernel(x, c, w_ada, b_ada, g_ffn1, w1_gate, w1_up, w1_down, g_mix, w_in, g_q_lat, w_uq, g_kv_lat, w_ukv, w_conv, a_log, dt_bias, g_gdn_out, w_o_mla, w_o_gdn, w_out, g_ffn2, w2_gate, w2_up, w2_down, g_final):
    batch, seq, d = x.shape
    depth = w_ada.shape[0]
    t = batch * seq
    half = MLA_ROPE // 2

    pos = jnp.arange(seq, dtype=F32)
    inv_freq = ROPE_THETA ** (-jnp.arange(0, MLA_ROPE, 2, dtype=F32) / MLA_ROPE)
    ang = pos[:, None] * inv_freq[None, :]
    cos, sin = jnp.cos(ang), jnp.sin(ang)
    cos_a = jnp.concatenate([cos, cos, cos, cos], axis=1)
    sin_a = jnp.concatenate([-sin, -sin, sin, sin], axis=1)

    o_q, o_kv, o_pe = 0, MLA_Q_RANK, MLA_Q_RANK + MLA_KV_RANK
    o_gq = o_pe + MLA_ROPE
    hw = GDN_HEADS * GDN_DK
    o_ga = o_gq + 3 * hw
    o_gb = o_ga + GDN_HEADS
    o_gz = o_gb + GDN_HEADS
    o_gates = o_gz + GDN_HEADS * GDN_DV

    c_pad = jnp.zeros((8, d), F32).at[:batch].set(c)
    xf = x.reshape(t, d)
    uq_perm = _uq_perm()
    ukv_perm = _ukv_perm()

    for l in range(depth):
        ada = _ada(c_pad, w_ada[l], b_ada[l][None, :])
        ada3 = ada[:batch].reshape(batch, N_MOD, d)

        xf = _ffn(xf, ada3, g_ffn1[l][None, :], w1_gate[l].astype(BF16), w1_up[l].astype(BF16),
                  w1_down[l].astype(BF16), g_final[None, :], mod=0, seq=seq, final_norm=False)

        wi = w_in[l]
        w_a = jnp.concatenate([wi[:, o_gq:o_ga], wi[:, o_gates:], wi[:, o_gz:o_gates]],
                              axis=1).astype(BF16)
        pe1 = wi[:, o_pe:o_pe + half]
        pe2 = wi[:, o_pe + half:o_gq]
        w_b = jnp.concatenate([wi[:, o_q:o_pe], pe1, pe1, pe2, pe2, wi[:, o_ga:o_gz],
                               jnp.zeros((d, LANE - 2 * GDN_HEADS), F32)], axis=1).astype(BF16)
        g_mix_l = g_mix[l][None, :]
        pa = _modmm(xf, ada3, g_mix_l, w_a, mod=1, seq=seq, out_dtype=BF16, tm=1024, tn=1024,
                    name="inproj_a")
        lat = _modmm(xf, ada3, g_mix_l, w_b, mod=1, seq=seq, out_dtype=F32, tm=1024,
                     tn=w_b.shape[1], name="inproj_b")

        q, k, v = _mla_prep(lat, g_q_lat[l][None, :], g_kv_lat[l][None, :],
                            w_uq[l][:, uq_perm].astype(BF16), w_ukv[l][:, ukv_perm].astype(BF16),
                            cos_a, sin_a, batch=batch, seq=seq)
        o_mla = _attention(q, k, v, batch=batch, seq=seq)

        prm = jnp.zeros((8, LANE), F32)
        prm = prm.at[0, :GDN_HEADS].set(a_log[l]).at[1, :GDN_HEADS].set(dt_bias[l])
        prm = prm.at[2, :].set(g_gdn_out[l])
        o_gdn = _gdn(pa, lat, w_conv[l], prm, batch=batch, seq=seq)

        xf = _merge(o_mla, o_gdn, pa, xf, ada3, w_o_mla[l].astype(BF16), w_o_gdn[l].astype(BF16),
                    w_out[l].astype(BF16), seq=seq)

        xf = _ffn(xf, ada3, g_ffn2[l][None, :], w2_gate[l].astype(BF16), w2_up[l].astype(BF16),
                  w2_down[l].astype(BF16), g_final[None, :], mod=2, seq=seq,
                  final_norm=(l == depth - 1))
    return xf.reshape(batch, seq, d)
```

```python
import functools

import numpy as np
import jax
import jax.numpy as jnp
from jax import lax
from jax.experimental import pallas as pl
from jax.experimental.pallas import tpu as pltpu

F32 = jnp.float32
BF16 = jnp.bfloat16

EPS = 1e-6
D_MODEL = 1024
D_FF = 2816
N_MOD = 9
CHUNK = 64
MLA_HEADS = 8
MLA_Q_RANK = 768
MLA_KV_RANK = 256
MLA_NOPE = 128
MLA_ROPE = 64
MLA_V = 128
ROPE_THETA = 10000.0
GDN_HEADS = 8
GDN_DK = 128
GDN_DV = 128
CONV_W = 4
LANE = 128
QK_PAD = 256
NEG_BIG = -1e30
VMEM_LIMIT = 56 * 1024 * 1024


def _cparams(sem):
    return pltpu.CompilerParams(dimension_semantics=sem, vmem_limit_bytes=VMEM_LIMIT)


def _sigmoid(x):
    return 1.0 / (1.0 + jnp.exp(-x))


def _silu(x):
    return x * _sigmoid(x)


def _rms(x):
    return x * lax.rsqrt(jnp.mean(x * x, axis=-1, keepdims=True) + EPS)


def _modulate(x, g, shift, scale):
    return (_rms(x) * g) * (1.0 + scale) + shift


def _ada_kernel(c_ref, w_ref, b_ref, o_ref):
    c = _silu(c_ref[...]).astype(BF16)
    o_ref[...] = jnp.dot(c, w_ref[...].astype(BF16), preferred_element_type=F32) + b_ref[...]


def _ada(c_pad, w_ada, b_ada):
    m, d = c_pad.shape
    n = w_ada.shape[1]
    tn = 1024
    return pl.pallas_call(
        _ada_kernel,
        grid=(n // tn,),
        in_specs=[pl.BlockSpec((m, d), lambda j: (0, 0)),
                  pl.BlockSpec((d, tn), lambda j: (0, j)),
                  pl.BlockSpec((1, tn), lambda j: (0, j))],
        out_specs=pl.BlockSpec((m, tn), lambda j: (0, j)),
        out_shape=jax.ShapeDtypeStruct((m, n), F32),
        compiler_params=_cparams(("parallel",)),
        name="ada",
    )(c_pad, w_ada, b_ada)


def _ffn_kernel(x_ref, ada_ref, g_ref, wg_ref, wu_ref, wd_ref, gf_ref, o_ref, *, mod, final_norm):
    x = x_ref[...]
    h = _modulate(x, g_ref[...], ada_ref[3 * mod:3 * mod + 1, :],
                  ada_ref[3 * mod + 1:3 * mod + 2, :]).astype(BF16)
    a = jnp.dot(h, wg_ref[...], preferred_element_type=F32)
    b = jnp.dot(h, wu_ref[...], preferred_element_type=F32)
    act = (_silu(a) * b).astype(BF16)
    gt = ada_ref[3 * mod + 2:3 * mod + 3, :]
    y = x + (0.5 * gt) * jnp.dot(act, wd_ref[...], preferred_element_type=F32)
    if final_norm:
        y = _rms(y) * gf_ref[...]
    o_ref[...] = y


def _resident(shape):
    return pl.BlockSpec(shape, lambda *_: (0,) * len(shape), pipeline_mode=pl.Buffered(1))


def _ffn(x, ada3, g, wg, wu, wd, gf, *, mod, seq, final_norm, tm=512):
    t, d = x.shape
    tiles_per_batch = seq // tm
    kern = functools.partial(_ffn_kernel, mod=mod, final_norm=final_norm)
    return pl.pallas_call(
        kern,
        grid=(t // tm,),
        in_specs=[pl.BlockSpec((tm, d), lambda i: (i, 0)),
                  pl.BlockSpec((None, N_MOD, d), lambda i: (i // tiles_per_batch, 0, 0)),
                  _resident((1, d)),
                  _resident(wg.shape), _resident(wu.shape), _resident(wd.shape),
                  _resident((1, d))],
        out_specs=pl.BlockSpec((tm, d), lambda i: (i, 0)),
        out_shape=jax.ShapeDtypeStruct((t, d), F32),
        compiler_params=_cparams(("parallel",)),
        name="ffn%d" % mod,
    )(x, ada3, g, wg, wu, wd, gf)


def _inproj_kernel(x_ref, ada_ref, g_ref, wa_ref, wb_ref, pa_ref, lat_ref, *, mod):
    h = _modulate(x_ref[...], g_ref[...], ada_ref[3 * mod:3 * mod + 1, :],
                  ada_ref[3 * mod + 1:3 * mod + 2, :]).astype(BF16)
    pa_ref[...] = jnp.dot(h, wa_ref[...], preferred_element_type=F32).astype(pa_ref.dtype)
    lat_ref[...] = jnp.dot(h, wb_ref[...], preferred_element_type=F32)


def _inproj(x, ada3, g, w_a, w_b, *, mod, seq, tm=512):
    t, d = x.shape
    na, nb = w_a.shape[1], w_b.shape[1]
    tiles_per_batch = seq // tm
    return pl.pallas_call(
        functools.partial(_inproj_kernel, mod=mod),
        grid=(t // tm,),
        in_specs=[pl.BlockSpec((tm, d), lambda i: (i, 0)),
                  pl.BlockSpec((None, N_MOD, d), lambda i: (i // tiles_per_batch, 0, 0)),
                  _resident((1, d)), _resident(w_a.shape), _resident(w_b.shape)],
        out_specs=[pl.BlockSpec((tm, na), lambda i: (i, 0)),
                   pl.BlockSpec((tm, nb), lambda i: (i, 0))],
        out_shape=[jax.ShapeDtypeStruct((t, na), BF16), jax.ShapeDtypeStruct((t, nb), F32)],
        compiler_params=_cparams(("parallel",)),
        name="inproj",
    )(x, ada3, g, w_a, w_b)


def _mla_prep_kernel(lat_ref, gq_ref, gkv_ref, wuq_ref, wukv_ref, cos_ref, sin_ref,
                     q_ref, k_ref, v_ref, *, scale):
    lat = lat_ref[...]
    qn = (_rms(lat[:, :MLA_Q_RANK]) * gq_ref[...]).astype(BF16)
    kvn = (_rms(lat[:, MLA_Q_RANK:MLA_Q_RANK + MLA_KV_RANK]) * gkv_ref[...]).astype(BF16)
    kpe = lat[:, MLA_Q_RANK + MLA_KV_RANK:MLA_Q_RANK + MLA_KV_RANK + LANE]
    q = jnp.dot(qn, wuq_ref[...], preferred_element_type=F32)
    kv = jnp.dot(kvn, wukv_ref[...], preferred_element_type=F32)
    cos = cos_ref[...]
    sin = sin_ref[...]
    kpe_r = (kpe * cos + pltpu.roll(kpe, LANE // 2, axis=1) * sin).astype(BF16)
    lane = lax.broadcasted_iota(jnp.int32, (1, LANE), 1)
    nope_w = MLA_HEADS * MLA_NOPE
    for p in range(MLA_HEADS // 2):
        pe = q[:, nope_w + LANE * p:nope_w + LANE * (p + 1)]
        pe_r = (pe * cos + pltpu.roll(pe, LANE // 2, axis=1) * sin) * scale
        for e in range(2):
            h = 2 * p + e
            keep = ((lane // 32) % 2) == e
            q_ref[h, :, 0:LANE] = (q[:, LANE * h:LANE * (h + 1)] * scale).astype(BF16)
            q_ref[h, :, LANE:2 * LANE] = jnp.where(keep, pe_r, 0.0).astype(BF16)
            k_ref[h, :, 0:LANE] = kv[:, LANE * h:LANE * (h + 1)].astype(BF16)
            k_ref[h, :, LANE:2 * LANE] = kpe_r
    v_ref[...] = kv[:, nope_w:].astype(BF16)


def _mla_prep(lat, gq, gkv, wuq, wukv, cos_a, sin_a, *, batch, seq, tm=512):
    t, nl = lat.shape
    tiles_per_batch = seq // tm
    scale = float((MLA_NOPE + MLA_ROPE) ** -0.5 * np.log2(np.e))
    qk_shape = jax.ShapeDtypeStruct((batch, MLA_HEADS, seq, QK_PAD), BF16)
    qk_spec = pl.BlockSpec((None, MLA_HEADS, tm, QK_PAD),
                           lambda i: (i // tiles_per_batch, 0, i % tiles_per_batch, 0))
    return pl.pallas_call(
        functools.partial(_mla_prep_kernel, scale=scale),
        grid=(t // tm,),
        in_specs=[pl.BlockSpec((tm, nl), lambda i: (i, 0)),
                  pl.BlockSpec((1, MLA_Q_RANK), lambda i: (0, 0)),
                  pl.BlockSpec((1, MLA_KV_RANK), lambda i: (0, 0)),
                  pl.BlockSpec(wuq.shape, lambda i: (0, 0)),
                  pl.BlockSpec(wukv.shape, lambda i: (0, 0)),
                  pl.BlockSpec((tm, LANE), lambda i: (i % tiles_per_batch, 0)),
                  pl.BlockSpec((tm, LANE), lambda i: (i % tiles_per_batch, 0))],
        out_specs=[qk_spec, qk_spec,
                   pl.BlockSpec((tm, MLA_HEADS * MLA_V), lambda i: (i, 0))],
        out_shape=[qk_shape, qk_shape,
                   jax.ShapeDtypeStruct((t, MLA_HEADS * MLA_V), BF16)],
        compiler_params=_cparams(("parallel",)),
        name="mla_prep",
    )(lat, gq, gkv, wuq, wukv, cos_a, sin_a)


def _attn_kernel(q_ref, k_ref, v_ref, o_ref, vt_ref, s_ref, p_ref, alpha_ref, m_ref, l_ref, acc_ref,
                 *, tq, tk):
    qi = pl.program_id(2)
    seq = v_ref.shape[0]
    assert tq == 2 * tk

    @pl.when(qi == 0)
    def _():
        for j in range(seq // tk):
            vt_ref[:, j * tk:(j + 1) * tk] = v_ref[j * tk:(j + 1) * tk, :].astype(F32).T.astype(BF16)

    q = q_ref[...]

    def scores(i, slot):
        start = pl.multiple_of(i * tk, tk)
        s_ref[slot] = lax.dot_general(k_ref[pl.ds(start, tk), :], q, (((1,), (1,)), ((), ())),
                                      preferred_element_type=F32)

    def softmax(slot, diag_off):
        s = s_ref[slot]
        if diag_off is not None:
            key = (lax.broadcasted_iota(jnp.int32, (tk, tq), 0) + diag_off) // CHUNK
            qry = lax.broadcasted_iota(jnp.int32, (tk, tq), 1) // CHUNK
            s = jnp.where(key <= qry, s, NEG_BIG)
        m = m_ref[...]
        m_new = jnp.maximum(m, jnp.max(s, axis=0, keepdims=True))
        p = jnp.exp2(s - m_new)
        alpha = jnp.exp2(m - m_new)
        l_ref[...] = alpha * l_ref[...] + jnp.sum(p, axis=0, keepdims=True)
        m_ref[...] = m_new
        alpha_ref[slot] = alpha
        p_ref[slot] = p.astype(BF16)

    def pv(i, slot):
        start = pl.multiple_of(i * tk, tk)
        acc_ref[...] = alpha_ref[slot] * acc_ref[...] + jnp.dot(
            vt_ref[:, pl.ds(start, tk)], p_ref[slot], preferred_element_type=F32)

    m_ref[...] = jnp.full(m_ref.shape, NEG_BIG, F32)
    l_ref[...] = jnp.zeros(l_ref.shape, F32)
    acc_ref[...] = jnp.zeros(acc_ref.shape, F32)
    p_ref[1] = jnp.zeros(p_ref.shape[1:], BF16)
    alpha_ref[1] = jnp.ones(alpha_ref.shape[1:], F32)
    scores(0, 0)

    def body(j, _):
        a = 2 * j
        pv(jnp.maximum(a - 1, 0), 1)
        scores(a + 1, 1)
        softmax(0, None)
        pv(a, 0)
        scores(a + 2, 0)
        softmax(1, None)
        return 0

    lax.fori_loop(0, qi, body, 0)
    a = 2 * qi
    pv(jnp.maximum(a - 1, 0), 1)
    scores(a + 1, 1)
    softmax(0, 0)
    pv(a, 0)
    softmax(1, tk)
    pv(a + 1, 1)
    o_ref[...] = (acc_ref[...] / l_ref[...]).T.astype(o_ref.dtype)


def _attention(q, k, v, *, batch, seq, tq=512):
    nq = seq // tq
    tk = tq // 2
    return pl.pallas_call(
        functools.partial(_attn_kernel, tq=tq, tk=tk),
        grid=(batch, MLA_HEADS, nq),
        in_specs=[pl.BlockSpec((None, None, tq, QK_PAD), lambda b, h, i: (b, h, i, 0)),
                  pl.BlockSpec((None, None, seq, QK_PAD), lambda b, h, i: (b, h, 0, 0)),
                  pl.BlockSpec((seq, MLA_V), lambda b, h, i: (b, h))],
        out_specs=pl.BlockSpec((tq, MLA_V), lambda b, h, i: (b * nq + i, h)),
        out_shape=jax.ShapeDtypeStruct((batch * seq, MLA_HEADS * MLA_V), BF16),
        scratch_shapes=[pltpu.VMEM((MLA_V, seq), BF16),
                        pltpu.VMEM((2, tk, tq), F32),
                        pltpu.VMEM((2, tk, tq), BF16),
                        pltpu.VMEM((2, 1, tq), F32),
                        pltpu.VMEM((1, tq), F32),
                        pltpu.VMEM((1, tq), F32),
                        pltpu.VMEM((MLA_V, tq), F32)],
        compiler_params=_cparams(("parallel", "parallel", "arbitrary")),
        name="attention",
    )(q, k, v)


def _split3(x):
    hi = x.astype(BF16)
    r = x - hi.astype(F32)
    mid = r.astype(BF16)
    lo = (r - mid.astype(F32)).astype(BF16)
    return hi, mid, lo


def _gdn_kernel(qkv_ref, z_ref, ab_ref, wconv_ref, prm_ref, o_ref, xbuf, ybuf, state, *, nc):
    n = pl.program_id(1)
    c = CHUNK
    rows = nc * c
    hw = GDN_HEADS * GDN_DK
    heads = range(GDN_HEADS)

    @pl.when(n == 0)
    def _():
        xbuf[0:8, :] = jnp.zeros((8, xbuf.shape[1]), F32)
        state[...] = jnp.zeros_like(state)

    xbuf[8:8 + rows, :] = qkv_ref[...].astype(F32)
    w = wconv_ref[...]
    y = w[3:4, :] * xbuf[8:8 + rows, :]
    for j in range(1, CONV_W):
        y = y + w[3 - j:4 - j, :] * xbuf[8 - j:8 - j + rows, :]
    ybuf[...] = _silu(y)
    xbuf[0:8, :] = xbuf[rows:rows + 8, :]

    ri = lax.broadcasted_iota(jnp.int32, (c, c), 0)
    ci = lax.broadcasted_iota(jnp.int32, (c, c), 1)
    causal = ri >= ci
    strict = ri > ci
    tri = jnp.where(causal, 1.0, 0.0).astype(BF16)
    eye = jnp.where(ri == ci, 1.0, 0.0)
    g_out = prm_ref[2:3, :]
    a_neg = -jnp.exp(prm_ref[0:1, :])
    nt = (((1,), (1,)), ((), ()))

    units = [(k, h) for k in range(nc) for h in heads]
    dec = []
    for k in range(nc):
        ab = ab_ref[k * c:(k + 1) * c, :]
        xg = ab + prm_ref[1:2, :]
        softplus = jnp.maximum(xg, 0.0) + jnp.log1p(jnp.exp(-jnp.abs(xg)))
        g_hi, g_mid, g_lo = _split3(a_neg * softplus)
        dcol = (jnp.dot(tri, g_hi, preferred_element_type=F32)
                + jnp.dot(tri, g_mid, preferred_element_type=F32)
                + jnp.dot(tri, g_lo, preferred_element_type=F32))
        dlast = dcol[c - 1:c, :]
        dec.append(dict(dcol=dcol, drow=dcol.T, beta=_sigmoid(ab), e=jnp.exp(dcol),
                        kdf=jnp.exp(dlast - dcol), cd=jnp.exp(dlast)))

    st = {}
    for (k, h) in units:
        r0 = k * c
        d = dec[k]
        qh = ybuf[r0:r0 + c, h * GDN_DK:(h + 1) * GDN_DK]
        kh = ybuf[r0:r0 + c, hw + h * GDN_DK:hw + (h + 1) * GDN_DK]
        vh = ybuf[r0:r0 + c, 2 * hw + h * GDN_DV:2 * hw + (h + 1) * GDN_DV]
        qh = qh * (lax.rsqrt(jnp.sum(qh * qh, axis=-1, keepdims=True) + EPS) * (GDN_DK ** -0.5))
        kh = kh * lax.rsqrt(jnp.sum(kh * kh, axis=-1, keepdims=True) + EPS)
        beta = d["beta"][:, GDN_HEADS + h:GDN_HEADS + h + 1]
        ed = d["e"][:, h:h + 1]
        kb = kh * beta
        kh16 = kh.astype(BF16)
        kk = lax.dot_general(kb.astype(BF16), kh16, nt, preferred_element_type=F32)
        qk = lax.dot_general(qh.astype(BF16), kh16, nt, preferred_element_type=F32)
        st[k, h] = dict(
            kk=kk, qk=qk,
            rhs=jnp.concatenate([kb * ed, vh * beta], axis=1).astype(BF16),
            qe=(qh * ed).astype(BF16),
            kd_t=(kh * d["kdf"][:, h:h + 1]).T.astype(BF16))
    for (k, h) in units:
        d, s = dec[k], st[k, h]
        lmat = jnp.exp(jnp.where(causal, d["dcol"][:, h:h + 1] - d["drow"][h:h + 1, :], NEG_BIG))
        a_mat = jnp.where(strict, s.pop("kk") * lmat, 0.0)
        s["attn"] = (s.pop("qk") * lmat).astype(BF16)
        s["pw"] = a_mat
        s["t"] = eye - a_mat
    for _ in range(5):
        for u in units:
            pw16 = st[u]["pw"].astype(BF16)
            st[u]["pw"] = jnp.dot(pw16, pw16, preferred_element_type=F32)
        for u in units:
            s = st[u]
            s["t"] = s["t"] + jnp.dot(s["t"].astype(BF16), s["pw"].astype(BF16),
                                      preferred_element_type=F32)
    for u in units:
        s = st[u]
        s["wu"] = jnp.dot(s.pop("t").astype(BF16), s.pop("rhs"), preferred_element_type=F32)

    for k in range(nc):
        r0 = k * c
        ws_qs = {}
        for h in heads:
            s = st[k, h]
            lhs = jnp.concatenate([s["wu"][:, :GDN_DK].astype(BF16), s["qe"]], axis=0)
            ws_qs[h] = jnp.dot(lhs, state[h].astype(BF16), preferred_element_type=F32)
        for h in heads:
            s = st[k, h]
            sl = slice(h * GDN_DV, (h + 1) * GDN_DV)
            v16 = (s["wu"][:, GDN_DK:] - ws_qs[h][:c, :]).astype(BF16)
            o = ws_qs[h][c:, :] + jnp.dot(s["attn"], v16, preferred_element_type=F32)
            state[h] = (state[h] * dec[k]["cd"][:, h:h + 1]
                        + jnp.dot(s["kd_t"], v16, preferred_element_type=F32))
            zh = z_ref[r0:r0 + c, sl].astype(F32)
            o_ref[r0:r0 + c, sl] = ((_rms(o) * g_out) * _silu(zh)).astype(o_ref.dtype)


def _gdn(pa, lat, wconv, prm, *, batch, seq, nc=4):
    rows = nc * CHUNK
    nstep = seq // rows
    hw3 = 3 * GDN_HEADS * GDN_DK
    hv = GDN_HEADS * GDN_DV
    ab_blk = (MLA_Q_RANK + MLA_KV_RANK + LANE) // LANE
    z_blk = pa.shape[1] // hv - 1
    return pl.pallas_call(
        functools.partial(_gdn_kernel, nc=nc),
        grid=(batch, nstep),
        in_specs=[pl.BlockSpec((rows, hw3), lambda b, n: (b * nstep + n, 0)),
                  pl.BlockSpec((rows, hv), lambda b, n: (b * nstep + n, z_blk)),
                  pl.BlockSpec((rows, LANE), lambda b, n: (b * nstep + n, ab_blk)),
                  pl.BlockSpec(wconv.shape, lambda b, n: (0, 0)),
                  pl.BlockSpec(prm.shape, lambda b, n: (0, 0))],
        out_specs=pl.BlockSpec((rows, hv), lambda b, n: (b * nstep + n, 0)),
        out_shape=jax.ShapeDtypeStruct((batch * seq, hv), BF16),
        scratch_shapes=[pltpu.VMEM((rows + 8, hw3), F32),
                        pltpu.VMEM((rows, hw3), F32),
                        pltpu.VMEM((GDN_HEADS, GDN_DK, GDN_DV), F32)],
        compiler_params=_cparams(("parallel", "arbitrary")),
        name="gdn",
    )(pa, pa, lat, wconv, prm)


def _merge_kernel(om_ref, og_ref, gm_ref, gg_ref, x_ref, ada_ref, wm_ref, wg_ref, wo_ref, o_ref):
    ym = jnp.dot(om_ref[...], wm_ref[...], preferred_element_type=F32)
    yg = jnp.dot(og_ref[...], wg_ref[...], preferred_element_type=F32)
    mix = _sigmoid(gm_ref[...].astype(F32)) * ym + _sigmoid(gg_ref[...].astype(F32)) * yg
    mixed = jnp.dot(mix.astype(BF16), wo_ref[...], preferred_element_type=F32)
    o_ref[...] = x_ref[...] + ada_ref[5:6, :] * mixed


def _merge(o_mla, o_gdn, pa, x1, ada3, wm, wg, wo, *, seq, tm=512):
    t, d = x1.shape
    tiles_per_batch = seq // tm
    gate_blk = 3 * GDN_HEADS * GDN_DK // d
    wspec = _resident((d, d))
    return pl.pallas_call(
        _merge_kernel,
        grid=(t // tm,),
        in_specs=[pl.BlockSpec((tm, d), lambda i: (i, 0)),
                  pl.BlockSpec((tm, d), lambda i: (i, 0)),
                  pl.BlockSpec((tm, d), lambda i: (i, gate_blk)),
                  pl.BlockSpec((tm, d), lambda i: (i, gate_blk + 1)),
                  pl.BlockSpec((tm, d), lambda i: (i, 0)),
                  pl.BlockSpec((None, N_MOD, d), lambda i: (i // tiles_per_batch, 0, 0)),
                  wspec, wspec, wspec],
        out_specs=pl.BlockSpec((tm, d), lambda i: (i, 0)),
        out_shape=jax.ShapeDtypeStruct((t, d), F32),
        compiler_params=_cparams(("parallel",)),
        name="merge",
    )(o_mla, o_gdn, pa, pa, x1, ada3, wm, wg, wo)


def _uq_perm():
    hd = MLA_NOPE + MLA_ROPE
    half = MLA_ROPE // 2
    cols = [h * hd + d for h in range(MLA_HEADS) for d in range(MLA_NOPE)]
    for p in range(MLA_HEADS // 2):
        for part in range(2):
            for e in range(2):
                h = 2 * p + e
                cols += [h * hd + MLA_NOPE + part * half + r for r in range(half)]
    return np.asarray(cols, np.int32)


def _ukv_perm():
    hd = MLA_NOPE + MLA_V
    k = [h * hd + d for h in range(MLA_HEADS) for d in range(MLA_NOPE)]
    v = [h * hd + MLA_NOPE + d for h in range(MLA_HEADS) for d in range(MLA_V)]
    return np.asarray(k + v, np.int32)


def kernel(x, c, w_ada, b_ada, g_ffn1, w1_gate, w1_up, w1_down, g_mix, w_in, g_q_lat, w_uq, g_kv_lat, w_ukv, w_conv, a_log, dt_bias, g_gdn_out, w_o_mla, w_o_gdn, w_out, g_ffn2, w2_gate, w2_up, w2_down, g_final):
    batch, seq, d = x.shape
    depth = w_ada.shape[0]
    t = batch * seq
    half = MLA_ROPE // 2

    pos = jnp.arange(seq, dtype=F32)
    inv_freq = ROPE_THETA ** (-jnp.arange(0, MLA_ROPE, 2, dtype=F32) / MLA_ROPE)
    ang = pos[:, None] * inv_freq[None, :]
    cos, sin = jnp.cos(ang), jnp.sin(ang)
    cos_a = jnp.concatenate([cos, cos, cos, cos], axis=1)
    sin_a = jnp.concatenate([-sin, -sin, sin, sin], axis=1)

    o_q, o_kv, o_pe = 0, MLA_Q_RANK, MLA_Q_RANK + MLA_KV_RANK
    o_gq = o_pe + MLA_ROPE
    hw = GDN_HEADS * GDN_DK
    o_ga = o_gq + 3 * hw
    o_gb = o_ga + GDN_HEADS
    o_gz = o_gb + GDN_HEADS
    o_gates = o_gz + GDN_HEADS * GDN_DV

    c_pad = jnp.zeros((8, d), F32).at[:batch].set(c)
    xf = x.reshape(t, d)
    uq_perm = _uq_perm()
    ukv_perm = _ukv_perm()

    for l in range(depth):
        ada = _ada(c_pad, w_ada[l], b_ada[l][None, :])
        ada3 = ada[:batch].reshape(batch, N_MOD, d)

        xf = _ffn(xf, ada3, g_ffn1[l][None, :], w1_gate[l].astype(BF16), w1_up[l].astype(BF16),
                  w1_down[l].astype(BF16), g_final[None, :], mod=0, seq=seq, final_norm=False)

        wi = w_in[l]
        w_a = jnp.concatenate([wi[:, o_gq:o_ga], wi[:, o_gates:], wi[:, o_gz:o_gates]],
                              axis=1).astype(BF16)
        pe1 = wi[:, o_pe:o_pe + half]
        pe2 = wi[:, o_pe + half:o_gq]
        w_b = jnp.concatenate([wi[:, o_q:o_pe], pe1, pe1, pe2, pe2, wi[:, o_ga:o_gz],
                               jnp.zeros((d, LANE - 2 * GDN_HEADS), F32)], axis=1).astype(BF16)
        g_mix_l = g_mix[l][None, :]
        pa, lat = _inproj(xf, ada3, g_mix_l, w_a, w_b, mod=1, seq=seq)

        q, k, v = _mla_prep(lat, g_q_lat[l][None, :], g_kv_lat[l][None, :],
                            w_uq[l][:, uq_perm].astype(BF16), w_ukv[l][:, ukv_perm].astype(BF16),
                            cos_a, sin_a, batch=batch, seq=seq)
        o_mla = _attention(q, k, v, batch=batch, seq=seq)

        prm = jnp.zeros((8, LANE), F32)
        prm = prm.at[0, :GDN_HEADS].set(a_log[l]).at[1, :GDN_HEADS].set(dt_bias[l])
        prm = prm.at[2, :].set(g_gdn_out[l])
        o_gdn = _gdn(pa, lat, w_conv[l], prm, batch=batch, seq=seq)

        xf = _merge(o_mla, o_gdn, pa, xf, ada3, w_o_mla[l].astype(BF16), w_o_gdn[l].astype(BF16),
                    w_out[l].astype(BF16), seq=seq)

        xf = _ffn(xf, ada3, g_ffn2[l][None, :], w2_gate[l].astype(BF16), w2_up[l].astype(BF16),
                  w2_down[l].astype(BF16), g_final[None, :], mod=2, seq=seq,
                  final_norm=(l == depth - 1))
    return xf.reshape(batch, seq, d)
```

```python
import functools

import numpy as np
import jax
import jax.numpy as jnp
from jax import lax
from jax.experimental import pallas as pl
from jax.experimental.pallas import tpu as pltpu

F32 = jnp.float32
BF16 = jnp.bfloat16

EPS = 1e-6
D_MODEL = 1024
D_FF = 2816
N_MOD = 9
CHUNK = 64
MLA_HEADS = 8
MLA_Q_RANK = 768
MLA_KV_RANK = 256
MLA_NOPE = 128
MLA_ROPE = 64
MLA_V = 128
ROPE_THETA = 10000.0
GDN_HEADS = 8
GDN_DK = 128
GDN_DV = 128
CONV_W = 4
LANE = 128
QK_PAD = 256
NEG_BIG = -1e30
VMEM_LIMIT = 56 * 1024 * 1024


def _cparams(sem):
    return pltpu.CompilerParams(dimension_semantics=sem, vmem_limit_bytes=VMEM_LIMIT)


def _sigmoid(x):
    return 1.0 / (1.0 + jnp.exp(-x))


def _silu(x):
    return x * _sigmoid(x)


def _rms(x):
    return x * lax.rsqrt(jnp.mean(x * x, axis=-1, keepdims=True) + EPS)


def _modulate(x, g, shift, scale):
    return (_rms(x) * g) * (1.0 + scale) + shift


def _ada_kernel(c_ref, w_ref, b_ref, o_ref):
    c = _silu(c_ref[...]).astype(BF16)
    o_ref[...] = jnp.dot(c, w_ref[...].astype(BF16), preferred_element_type=F32) + b_ref[...]


def _ada(c_pad, w_ada, b_ada):
    m, d = c_pad.shape
    n = w_ada.shape[1]
    tn = 1024
    return pl.pallas_call(
        _ada_kernel,
        grid=(n // tn,),
        in_specs=[pl.BlockSpec((m, d), lambda j: (0, 0)),
                  pl.BlockSpec((d, tn), lambda j: (0, j)),
                  pl.BlockSpec((1, tn), lambda j: (0, j))],
        out_specs=pl.BlockSpec((m, tn), lambda j: (0, j)),
        out_shape=jax.ShapeDtypeStruct((m, n), F32),
        compiler_params=_cparams(("parallel",)),
        name="ada",
    )(c_pad, w_ada, b_ada)


def _ffn_kernel(x_ref, ada_ref, g_ref, wg_ref, wu_ref, wd_ref, gf_ref, o_ref, *, mod, final_norm):
    x = x_ref[...]
    h = _modulate(x, g_ref[...], ada_ref[3 * mod:3 * mod + 1, :],
                  ada_ref[3 * mod + 1:3 * mod + 2, :]).astype(BF16)
    a = jnp.dot(h, wg_ref[...], preferred_element_type=F32)
    b = jnp.dot(h, wu_ref[...], preferred_element_type=F32)
    act = (_silu(a) * b).astype(BF16)
    gt = ada_ref[3 * mod + 2:3 * mod + 3, :]
    y = x + (0.5 * gt) * jnp.dot(act, wd_ref[...], preferred_element_type=F32)
    if final_norm:
        y = _rms(y) * gf_ref[...]
    o_ref[...] = y


def _resident(shape):
    return pl.BlockSpec(shape, lambda *_: (0,) * len(shape), pipeline_mode=pl.Buffered(1))


def _ffn(x, ada3, g, wg, wu, wd, gf, *, mod, seq, final_norm, tm=512):
    t, d = x.shape
    tiles_per_batch = seq // tm
    kern = functools.partial(_ffn_kernel, mod=mod, final_norm=final_norm)
    return pl.pallas_call(
        kern,
        grid=(t // tm,),
        in_specs=[pl.BlockSpec((tm, d), lambda i: (i, 0)),
                  pl.BlockSpec((None, N_MOD, d), lambda i: (i // tiles_per_batch, 0, 0)),
                  _resident((1, d)),
                  _resident(wg.shape), _resident(wu.shape), _resident(wd.shape),
                  _resident((1, d))],
        out_specs=pl.BlockSpec((tm, d), lambda i: (i, 0)),
        out_shape=jax.ShapeDtypeStruct((t, d), F32),
        compiler_params=_cparams(("parallel",)),
        name="ffn%d" % mod,
    )(x, ada3, g, wg, wu, wd, gf)


def _inproj_kernel(x_ref, ada_ref, g_ref, wa_ref, wb_ref, pa_ref, lat_ref, *, mod):
    h = _modulate(x_ref[...], g_ref[...], ada_ref[3 * mod:3 * mod + 1, :],
                  ada_ref[3 * mod + 1:3 * mod + 2, :]).astype(BF16)
    pa_ref[...] = jnp.dot(h, wa_ref[...], preferred_element_type=F32).astype(pa_ref.dtype)
    lat_ref[...] = jnp.dot(h, wb_ref[...], preferred_element_type=F32)


def _inproj(x, ada3, g, w_a, w_b, *, mod, seq, tm=512):
    t, d = x.shape
    na, nb = w_a.shape[1], w_b.shape[1]
    tiles_per_batch = seq // tm
    return pl.pallas_call(
        functools.partial(_inproj_kernel, mod=mod),
        grid=(t // tm,),
        in_specs=[pl.BlockSpec((tm, d), lambda i: (i, 0)),
                  pl.BlockSpec((None, N_MOD, d), lambda i: (i // tiles_per_batch, 0, 0)),
                  _resident((1, d)), _resident(w_a.shape), _resident(w_b.shape)],
        out_specs=[pl.BlockSpec((tm, na), lambda i: (i, 0)),
                   pl.BlockSpec((tm, nb), lambda i: (i, 0))],
        out_shape=[jax.ShapeDtypeStruct((t, na), BF16), jax.ShapeDtypeStruct((t, nb), F32)],
        compiler_params=_cparams(("parallel",)),
        name="inproj",
    )(x, ada3, g, w_a, w_b)


def _mla_prep_kernel(lat_ref, gq_ref, gkv_ref, wuq_ref, wukv_ref, cos_ref, sin_ref,
                     q_ref, k_ref, v_ref, *, scale):
    lat = lat_ref[...]
    qn = (_rms(lat[:, :MLA_Q_RANK]) * gq_ref[...]).astype(BF16)
    kvn = (_rms(lat[:, MLA_Q_RANK:MLA_Q_RANK + MLA_KV_RANK]) * gkv_ref[...]).astype(BF16)
    kpe = lat[:, MLA_Q_RANK + MLA_KV_RANK:MLA_Q_RANK + MLA_KV_RANK + LANE]
    q = jnp.dot(qn, wuq_ref[...], preferred_element_type=F32)
    kv = jnp.dot(kvn, wukv_ref[...], preferred_element_type=F32)
    cos = cos_ref[...]
    sin = sin_ref[...]
    kpe_r = (kpe * cos + pltpu.roll(kpe, LANE // 2, axis=1) * sin).astype(BF16)
    lane = lax.broadcasted_iota(jnp.int32, (1, LANE), 1)
    nope_w = MLA_HEADS * MLA_NOPE
    for p in range(MLA_HEADS // 2):
        pe = q[:, nope_w + LANE * p:nope_w + LANE * (p + 1)]
        pe_r = (pe * cos + pltpu.roll(pe, LANE // 2, axis=1) * sin) * scale
        for e in range(2):
            h = 2 * p + e
            keep = ((lane // 32) % 2) == e
            q_ref[h, :, 0:LANE] = (q[:, LANE * h:LANE * (h + 1)] * scale).astype(BF16)
            q_ref[h, :, LANE:2 * LANE] = jnp.where(keep, pe_r, 0.0).astype(BF16)
            k_ref[h, :, 0:LANE] = kv[:, LANE * h:LANE * (h + 1)].astype(BF16)
            k_ref[h, :, LANE:2 * LANE] = kpe_r
    v_ref[...] = kv[:, nope_w:].T.astype(BF16)


def _mla_prep(lat, gq, gkv, wuq, wukv, cos_a, sin_a, *, batch, seq, tm=512):
    t, nl = lat.shape
    tiles_per_batch = seq // tm
    scale = float((MLA_NOPE + MLA_ROPE) ** -0.5 * np.log2(np.e))
    qk_shape = jax.ShapeDtypeStruct((batch, MLA_HEADS, seq, QK_PAD), BF16)
    qk_spec = pl.BlockSpec((None, MLA_HEADS, tm, QK_PAD),
                           lambda i: (i // tiles_per_batch, 0, i % tiles_per_batch, 0))
    return pl.pallas_call(
        functools.partial(_mla_prep_kernel, scale=scale),
        grid=(t // tm,),
        in_specs=[pl.BlockSpec((tm, nl), lambda i: (i, 0)),
                  pl.BlockSpec((1, MLA_Q_RANK), lambda i: (0, 0)),
                  pl.BlockSpec((1, MLA_KV_RANK), lambda i: (0, 0)),
                  pl.BlockSpec(wuq.shape, lambda i: (0, 0)),
                  pl.BlockSpec(wukv.shape, lambda i: (0, 0)),
                  pl.BlockSpec((tm, LANE), lambda i: (i % tiles_per_batch, 0)),
                  pl.BlockSpec((tm, LANE), lambda i: (i % tiles_per_batch, 0))],
        out_specs=[qk_spec, qk_spec,
                   pl.BlockSpec((None, MLA_HEADS * MLA_V, tm),
                                lambda i: (i // tiles_per_batch, 0, i % tiles_per_batch))],
        out_shape=[qk_shape, qk_shape,
                   jax.ShapeDtypeStruct((batch, MLA_HEADS * MLA_V, seq), BF16)],
        compiler_params=_cparams(("parallel",)),
        name="mla_prep",
    )(lat, gq, gkv, wuq, wukv, cos_a, sin_a)


def _attn_kernel(q_ref, k_ref, vt_ref, o_ref, s_ref, p_ref, alpha_ref, m_ref, l_ref, acc_ref, *, tq, tk):
    qi = pl.program_id(2)
    assert tq == 2 * tk
    full = slice(0, tq)
    upper = slice(tk, tq)

    def scores(i, slot, qs=full):
        start = pl.multiple_of(i * tk, tk)
        s_ref[slot, :, qs] = lax.dot_general(k_ref[pl.ds(start, tk), :], q_ref[qs, :],
                                             (((1,), (1,)), ((), ())), preferred_element_type=F32)

    def softmax(slot, qs=full, diagonal=False):
        s = s_ref[slot, :, qs]
        if diagonal:
            key = lax.broadcasted_iota(jnp.int32, (tk, 1), 0) // CHUNK
            qry = lax.broadcasted_iota(jnp.int32, (1, s.shape[1]), 1) // CHUNK
            s = jnp.where(key <= qry, s, NEG_BIG)
        m = m_ref[:, qs]
        m_new = jnp.maximum(m, jnp.max(s, axis=0, keepdims=True))
        p = jnp.exp2(s - m_new)
        alpha = jnp.exp2(m - m_new)
        l_ref[:, qs] = alpha * l_ref[:, qs] + jnp.sum(p, axis=0, keepdims=True)
        m_ref[:, qs] = m_new
        alpha_ref[slot, :, qs] = alpha
        p_ref[slot, :, qs] = p.astype(BF16)

    def pv(i, slot, qs=full):
        start = pl.multiple_of(i * tk, tk)
        acc_ref[:, qs] = alpha_ref[slot, :, qs] * acc_ref[:, qs] + jnp.dot(
            vt_ref[:, pl.ds(start, tk)], p_ref[slot, :, qs], preferred_element_type=F32)

    m_ref[...] = jnp.full(m_ref.shape, NEG_BIG, F32)
    l_ref[...] = jnp.zeros(l_ref.shape, F32)
    acc_ref[...] = jnp.zeros(acc_ref.shape, F32)
    p_ref[1] = jnp.zeros(p_ref.shape[1:], BF16)
    alpha_ref[1] = jnp.ones(alpha_ref.shape[1:], F32)
    scores(0, 0)

    def body(j, _):
        a = 2 * j
        pv(jnp.maximum(a - 1, 0), 1)
        scores(a + 1, 1)
        softmax(0)
        pv(a, 0)
        scores(a + 2, 0)
        softmax(1)
        return 0

    lax.fori_loop(0, qi, body, 0)
    a = 2 * qi
    pv(jnp.maximum(a - 1, 0), 1)
    scores(a + 1, 1, upper)
    softmax(0, full, diagonal=True)
    pv(a, 0)
    softmax(1, upper, diagonal=True)
    pv(a + 1, 1, upper)
    o_ref[...] = (acc_ref[...] * (1.0 / l_ref[...])).T.astype(o_ref.dtype)


def _attention(q, k, vt, *, batch, seq, tq=1024):
    nq = seq // tq
    tk = tq // 2
    return pl.pallas_call(
        functools.partial(_attn_kernel, tq=tq, tk=tk),
        grid=(batch, MLA_HEADS, nq),
        in_specs=[pl.BlockSpec((None, None, tq, QK_PAD), lambda b, h, i: (b, h, i, 0)),
                  pl.BlockSpec((None, None, seq, QK_PAD), lambda b, h, i: (b, h, 0, 0)),
                  pl.BlockSpec((None, MLA_V, seq), lambda b, h, i: (b, h, 0))],
        out_specs=pl.BlockSpec((tq, MLA_V), lambda b, h, i: (b * nq + i, h)),
        out_shape=jax.ShapeDtypeStruct((batch * seq, MLA_HEADS * MLA_V), BF16),
        scratch_shapes=[pltpu.VMEM((2, tk, tq), F32),
                        pltpu.VMEM((2, tk, tq), BF16),
                        pltpu.VMEM((2, 1, tq), F32),
                        pltpu.VMEM((1, tq), F32),
                        pltpu.VMEM((1, tq), F32),
                        pltpu.VMEM((MLA_V, tq), F32)],
        compiler_params=_cparams(("parallel", "parallel", "arbitrary")),
        name="attention",
    )(q, k, vt)


def _split3(x):
    hi = x.astype(BF16)
    r = x - hi.astype(F32)
    mid = r.astype(BF16)
    lo = (r - mid.astype(F32)).astype(BF16)
    return hi, mid, lo


def _gdn_kernel(qkv_ref, z_ref, ab_ref, wconv_ref, prm_ref, o_ref, xbuf, ybuf, state, *, nc):
    n = pl.program_id(1)
    c = CHUNK
    rows = nc * c
    hw = GDN_HEADS * GDN_DK
    heads = range(GDN_HEADS)

    @pl.when(n == 0)
    def _():
        xbuf[0:8, :] = jnp.zeros((8, xbuf.shape[1]), F32)
        state[...] = jnp.zeros_like(state)

    xbuf[8:8 + rows, :] = qkv_ref[...].astype(F32)
    w = wconv_ref[...]
    y = w[3:4, :] * xbuf[8:8 + rows, :]
    for j in range(1, CONV_W):
        y = y + w[3 - j:4 - j, :] * xbuf[8 - j:8 - j + rows, :]
    ybuf[...] = _silu(y)
    xbuf[0:8, :] = xbuf[rows:rows + 8, :]

    ri = lax.broadcasted_iota(jnp.int32, (c, c), 0)
    ci = lax.broadcasted_iota(jnp.int32, (c, c), 1)
    causal = ri >= ci
    strict = ri > ci
    tri = jnp.where(causal, 1.0, 0.0).astype(BF16)
    eye = jnp.where(ri == ci, 1.0, 0.0)
    g_out = prm_ref[2:3, :]
    a_neg = -jnp.exp(prm_ref[0:1, :])
    nt = (((1,), (1,)), ((), ()))

    units = [(k, h) for k in range(nc) for h in heads]
    dec = []
    for k in range(nc):
        ab = ab_ref[k * c:(k + 1) * c, :]
        xg = ab + prm_ref[1:2, :]
        softplus = jnp.maximum(xg, 0.0) + jnp.log1p(jnp.exp(-jnp.abs(xg)))
        g_hi, g_mid, g_lo = _split3(a_neg * softplus)
        dcol = (jnp.dot(tri, g_hi, preferred_element_type=F32)
                + jnp.dot(tri, g_mid, preferred_element_type=F32)
                + jnp.dot(tri, g_lo, preferred_element_type=F32))
        dlast = dcol[c - 1:c, :]
        dec.append(dict(dcol=dcol, drow=dcol.T, beta=_sigmoid(ab), e=jnp.exp(dcol),
                        kdf=jnp.exp(dlast - dcol), cd=jnp.exp(dlast)))

    st = {}
    for (k, h) in units:
        r0 = k * c
        d = dec[k]
        qh = ybuf[r0:r0 + c, h * GDN_DK:(h + 1) * GDN_DK]
        kh = ybuf[r0:r0 + c, hw + h * GDN_DK:hw + (h + 1) * GDN_DK]
        vh = ybuf[r0:r0 + c, 2 * hw + h * GDN_DV:2 * hw + (h + 1) * GDN_DV]
        qh = qh * (lax.rsqrt(jnp.sum(qh * qh, axis=-1, keepdims=True) + EPS) * (GDN_DK ** -0.5))
        kh = kh * lax.rsqrt(jnp.sum(kh * kh, axis=-1, keepdims=True) + EPS)
        beta = d["beta"][:, GDN_HEADS + h:GDN_HEADS + h + 1]
        ed = d["e"][:, h:h + 1]
        kb = kh * beta
        kh16 = kh.astype(BF16)
        kk = lax.dot_general(kb.astype(BF16), kh16, nt, preferred_element_type=F32)
        qk = lax.dot_general(qh.astype(BF16), kh16, nt, preferred_element_type=F32)
        st[k, h] = dict(
            kk=kk, qk=qk,
            rhs=jnp.concatenate([kb * ed, vh * beta], axis=1).astype(BF16),
            qe=(qh * ed).astype(BF16),
            kd_t=(kh * d["kdf"][:, h:h + 1]).T.astype(BF16))
    for (k, h) in units:
        d, s = dec[k], st[k, h]
        lmat = jnp.exp(jnp.where(causal, d["dcol"][:, h:h + 1] - d["drow"][h:h + 1, :], NEG_BIG))
        a_mat = jnp.where(strict, s.pop("kk") * lmat, 0.0)
        s["attn"] = (s.pop("qk") * lmat).astype(BF16)
        s["pw"] = a_mat
        s["t"] = eye - a_mat
    for _ in range(5):
        for u in units:
            pw16 = st[u]["pw"].astype(BF16)
            st[u]["pw"] = jnp.dot(pw16, pw16, preferred_element_type=F32)
        for u in units:
            s = st[u]
            s["t"] = s["t"] + jnp.dot(s["t"].astype(BF16), s["pw"].astype(BF16),
                                      preferred_element_type=F32)
    for u in units:
        s = st[u]
        s["wu"] = jnp.dot(s.pop("t").astype(BF16), s.pop("rhs"), preferred_element_type=F32)

    for k in range(nc):
        r0 = k * c
        ws_qs = {}
        for h in heads:
            s = st[k, h]
            lhs = jnp.concatenate([s["wu"][:, :GDN_DK].astype(BF16), s["qe"]], axis=0)
            ws_qs[h] = jnp.dot(lhs, state[h].astype(BF16), preferred_element_type=F32)
        for h in heads:
            s = st[k, h]
            sl = slice(h * GDN_DV, (h + 1) * GDN_DV)
            v16 = (s["wu"][:, GDN_DK:] - ws_qs[h][:c, :]).astype(BF16)
            o = ws_qs[h][c:, :] + jnp.dot(s["attn"], v16, preferred_element_type=F32)
            state[h] = (state[h] * dec[k]["cd"][:, h:h + 1]
                        + jnp.dot(s["kd_t"], v16, preferred_element_type=F32))
            zh = z_ref[r0:r0 + c, sl].astype(F32)
            o_ref[r0:r0 + c, sl] = ((_rms(o) * g_out) * _silu(zh)).astype(o_ref.dtype)


def _gdn(pa, lat, wconv, prm, *, batch, seq, nc=4):
    rows = nc * CHUNK
    nstep = seq // rows
    hw3 = 3 * GDN_HEADS * GDN_DK
    hv = GDN_HEADS * GDN_DV
    ab_blk = (MLA_Q_RANK + MLA_KV_RANK + LANE) // LANE
    z_blk = pa.shape[1] // hv - 1
    return pl.pallas_call(
        functools.partial(_gdn_kernel, nc=nc),
        grid=(batch, nstep),
        in_specs=[pl.BlockSpec((rows, hw3), lambda b, n: (b * nstep + n, 0)),
                  pl.BlockSpec((rows, hv), lambda b, n: (b * nstep + n, z_blk)),
                  pl.BlockSpec((rows, LANE), lambda b, n: (b * nstep + n, ab_blk)),
                  pl.BlockSpec(wconv.shape, lambda b, n: (0, 0)),
                  pl.BlockSpec(prm.shape, lambda b, n: (0, 0))],
        out_specs=pl.BlockSpec((rows, hv), lambda b, n: (b * nstep + n, 0)),
        out_shape=jax.ShapeDtypeStruct((batch * seq, hv), BF16),
        scratch_shapes=[pltpu.VMEM((rows + 8, hw3), F32),
                        pltpu.VMEM((rows, hw3), F32),
                        pltpu.VMEM((GDN_HEADS, GDN_DK, GDN_DV), F32)],
        compiler_params=_cparams(("parallel", "arbitrary")),
        name="gdn",
    )(pa, pa, lat, wconv, prm)


def _merge_kernel(om_ref, og_ref, gm_ref, gg_ref, x_ref, ada_ref, wm_ref, wg_ref, wo_ref, o_ref):
    ym = jnp.dot(om_ref[...], wm_ref[...], preferred_element_type=F32)
    yg = jnp.dot(og_ref[...], wg_ref[...], preferred_element_type=F32)
    mix = _sigmoid(gm_ref[...].astype(F32)) * ym + _sigmoid(gg_ref[...].astype(F32)) * yg
    mixed = jnp.dot(mix.astype(BF16), wo_ref[...], preferred_element_type=F32)
    o_ref[...] = x_ref[...] + ada_ref[5:6, :] * mixed


def _merge(o_mla, o_gdn, pa, x1, ada3, wm, wg, wo, *, seq, tm=512):
    t, d = x1.shape
    tiles_per_batch = seq // tm
    gate_blk = 3 * GDN_HEADS * GDN_DK // d
    wspec = _resident((d, d))
    return pl.pallas_call(
        _merge_kernel,
        grid=(t // tm,),
        in_specs=[pl.BlockSpec((tm, d), lambda i: (i, 0)),
                  pl.BlockSpec((tm, d), lambda i: (i, 0)),
                  pl.BlockSpec((tm, d), lambda i: (i, gate_blk)),
                  pl.BlockSpec((tm, d), lambda i: (i, gate_blk + 1)),
                  pl.BlockSpec((tm, d), lambda i: (i, 0)),
                  pl.BlockSpec((None, N_MOD, d), lambda i: (i // tiles_per_batch, 0, 0)),
                  wspec, wspec, wspec],
        out_specs=pl.BlockSpec((tm, d), lambda i: (i, 0)),
        out_shape=jax.ShapeDtypeStruct((t, d), F32),
        compiler_params=_cparams(("parallel",)),
        name="merge",
    )(o_mla, o_gdn, pa, pa, x1, ada3, wm, wg, wo)


def _uq_perm():
    hd = MLA_NOPE + MLA_ROPE
    half = MLA_ROPE // 2
    cols = [h * hd + d for h in range(MLA_HEADS) for d in range(MLA_NOPE)]
    for p in range(MLA_HEADS // 2):
        for part in range(2):
            for e in range(2):
                h = 2 * p + e
                cols += [h * hd + MLA_NOPE + part * half + r for r in range(half)]
    return np.asarray(cols, np.int32)


def _ukv_perm():
    hd = MLA_NOPE + MLA_V
    k = [h * hd + d for h in range(MLA_HEADS) for d in range(MLA_NOPE)]
    v = [h * hd + MLA_NOPE + d for h in range(MLA_HEADS) for d in range(MLA_V)]
    return np.asarray(k + v, np.int32)


def kernel(x, c, w_ada, b_ada, g_ffn1, w1_gate, w1_up, w1_down, g_mix, w_in, g_q_lat, w_uq, g_kv_lat, w_ukv, w_conv, a_log, dt_bias, g_gdn_out, w_o_mla, w_o_gdn, w_out, g_ffn2, w2_gate, w2_up, w2_down, g_final):
    batch, seq, d = x.shape
    depth = w_ada.shape[0]
    t = batch * seq
    half = MLA_ROPE // 2

    pos = jnp.arange(seq, dtype=F32)
    inv_freq = ROPE_THETA ** (-jnp.arange(0, MLA_ROPE, 2, dtype=F32) / MLA_ROPE)
    ang = pos[:, None] * inv_freq[None, :]
    cos, sin = jnp.cos(ang), jnp.sin(ang)
    cos_a = jnp.concatenate([cos, cos, cos, cos], axis=1)
    sin_a = jnp.concatenate([-sin, -sin, sin, sin], axis=1)

    o_q, o_kv, o_pe = 0, MLA_Q_RANK, MLA_Q_RANK + MLA_KV_RANK
    o_gq = o_pe + MLA_ROPE
    hw = GDN_HEADS * GDN_DK
    o_ga = o_gq + 3 * hw
    o_gb = o_ga + GDN_HEADS
    o_gz = o_gb + GDN_HEADS
    o_gates = o_gz + GDN_HEADS * GDN_DV

    c_pad = jnp.zeros((8, d), F32).at[:batch].set(c)
    xf = x.reshape(t, d)
    uq_perm = _uq_perm()
    ukv_perm = _ukv_perm()

    for l in range(depth):
        ada = _ada(c_pad, w_ada[l], b_ada[l][None, :])
        ada3 = ada[:batch].reshape(batch, N_MOD, d)

        xf = _ffn(xf, ada3, g_ffn1[l][None, :], w1_gate[l].astype(BF16), w1_up[l].astype(BF16),
                  w1_down[l].astype(BF16), g_final[None, :], mod=0, seq=seq, final_norm=False)

        wi = w_in[l]
        w_a = jnp.concatenate([wi[:, o_gq:o_ga], wi[:, o_gates:], wi[:, o_gz:o_gates]],
                              axis=1).astype(BF16)
        pe1 = wi[:, o_pe:o_pe + half]
        pe2 = wi[:, o_pe + half:o_gq]
        w_b = jnp.concatenate([wi[:, o_q:o_pe], pe1, pe1, pe2, pe2, wi[:, o_ga:o_gz],
                               jnp.zeros((d, LANE - 2 * GDN_HEADS), F32)], axis=1).astype(BF16)
        g_mix_l = g_mix[l][None, :]
        pa, lat = _inproj(xf, ada3, g_mix_l, w_a, w_b, mod=1, seq=seq)

        q, k, vt = _mla_prep(lat, g_q_lat[l][None, :], g_kv_lat[l][None, :],
                            w_uq[l][:, uq_perm].astype(BF16), w_ukv[l][:, ukv_perm].astype(BF16),
                            cos_a, sin_a, batch=batch, seq=seq)
        o_mla = _attention(q, k, vt, batch=batch, seq=seq)

        prm = jnp.zeros((8, LANE), F32)
        prm = prm.at[0, :GDN_HEADS].set(a_log[l]).at[1, :GDN_HEADS].set(dt_bias[l])
        prm = prm.at[2, :].set(g_gdn_out[l])
        o_gdn = _gdn(pa, lat, w_conv[l], prm, batch=batch, seq=seq)

        xf = _merge(o_mla, o_gdn, pa, xf, ada3, w_o_mla[l].astype(BF16), w_o_gdn[l].astype(BF16),
                    w_out[l].astype(BF16), seq=seq)

        xf = _ffn(xf, ada3, g_ffn2[l][None, :], w2_gate[l].astype(BF16), w2_up[l].astype(BF16),
                  w2_down[l].astype(BF16), g_final[None, :], mod=2, seq=seq,
                  final_norm=(l == depth - 1))
    return xf.reshape(batch, seq, d)
```

```python
import functools

import numpy as np
import jax
import jax.numpy as jnp
from jax import lax
from jax.experimental import pallas as pl
from jax.experimental.pallas import tpu as pltpu

F32 = jnp.float32
BF16 = jnp.bfloat16

EPS = 1e-6
D_MODEL = 1024
D_FF = 2816
N_MOD = 9
CHUNK = 64
MLA_HEADS = 8
MLA_Q_RANK = 768
MLA_KV_RANK = 256
MLA_NOPE = 128
MLA_ROPE = 64
MLA_V = 128
ROPE_THETA = 10000.0
GDN_HEADS = 8
GDN_DK = 128
GDN_DV = 128
CONV_W = 4
LANE = 128
QK_PAD = 256
NEG_BIG = -1e30
VMEM_LIMIT = 56 * 1024 * 1024


def _cparams(sem):
    return pltpu.CompilerParams(dimension_semantics=sem, vmem_limit_bytes=VMEM_LIMIT)


def _sigmoid(x):
    return 1.0 / (1.0 + jnp.exp(-x))


def _silu(x):
    return x * _sigmoid(x)


def _rms(x):
    return x * lax.rsqrt(jnp.mean(x * x, axis=-1, keepdims=True) + EPS)


def _modulate(x, g, shift, scale):
    return (_rms(x) * g) * (1.0 + scale) + shift


def _ada_kernel(c_ref, w_ref, b_ref, o_ref):
    c = _silu(c_ref[...]).astype(BF16)
    o_ref[...] = jnp.dot(c, w_ref[...].astype(BF16), preferred_element_type=F32) + b_ref[...]


def _ada(c_pad, w_ada, b_ada):
    m, d = c_pad.shape
    n = w_ada.shape[1]
    tn = 1024
    return pl.pallas_call(
        _ada_kernel,
        grid=(n // tn,),
        in_specs=[pl.BlockSpec((m, d), lambda j: (0, 0)),
                  pl.BlockSpec((d, tn), lambda j: (0, j)),
                  pl.BlockSpec((1, tn), lambda j: (0, j))],
        out_specs=pl.BlockSpec((m, tn), lambda j: (0, j)),
        out_shape=jax.ShapeDtypeStruct((m, n), F32),
        compiler_params=_cparams(("parallel",)),
        name="ada",
    )(c_pad, w_ada, b_ada)


def _ffn_kernel(x_ref, ada_ref, g_ref, wg_ref, wu_ref, wd_ref, gf_ref, o_ref, *, mod, final_norm):
    x = x_ref[...]
    h = _modulate(x, g_ref[...], ada_ref[3 * mod:3 * mod + 1, :],
                  ada_ref[3 * mod + 1:3 * mod + 2, :]).astype(BF16)
    a = jnp.dot(h, wg_ref[...], preferred_element_type=F32)
    b = jnp.dot(h, wu_ref[...], preferred_element_type=F32)
    act = (_silu(a) * b).astype(BF16)
    gt = ada_ref[3 * mod + 2:3 * mod + 3, :]
    y = x + (0.5 * gt) * jnp.dot(act, wd_ref[...], preferred_element_type=F32)
    if final_norm:
        y = _rms(y) * gf_ref[...]
    o_ref[...] = y


def _resident(shape):
    return pl.BlockSpec(shape, lambda *_: (0,) * len(shape), pipeline_mode=pl.Buffered(1))


def _ffn(x, ada3, g, wg, wu, wd, gf, *, mod, seq, final_norm, tm=512):
    t, d = x.shape
    tiles_per_batch = seq // tm
    kern = functools.partial(_ffn_kernel, mod=mod, final_norm=final_norm)
    return pl.pallas_call(
        kern,
        grid=(t // tm,),
        in_specs=[pl.BlockSpec((tm, d), lambda i: (i, 0)),
                  pl.BlockSpec((None, N_MOD, d), lambda i: (i // tiles_per_batch, 0, 0)),
                  _resident((1, d)),
                  _resident(wg.shape), _resident(wu.shape), _resident(wd.shape),
                  _resident((1, d))],
        out_specs=pl.BlockSpec((tm, d), lambda i: (i, 0)),
        out_shape=jax.ShapeDtypeStruct((t, d), F32),
        compiler_params=_cparams(("parallel",)),
        name="ffn%d" % mod,
    )(x, ada3, g, wg, wu, wd, gf)


def _inproj_kernel(x_ref, ada_ref, g_ref, wa_ref, wb_ref, pa_ref, lat_ref, *, mod):
    h = _modulate(x_ref[...], g_ref[...], ada_ref[3 * mod:3 * mod + 1, :],
                  ada_ref[3 * mod + 1:3 * mod + 2, :]).astype(BF16)
    pa_ref[...] = jnp.dot(h, wa_ref[...], preferred_element_type=F32).astype(pa_ref.dtype)
    lat_ref[...] = jnp.dot(h, wb_ref[...], preferred_element_type=F32)


def _inproj(x, ada3, g, w_a, w_b, *, mod, seq, tm=512):
    t, d = x.shape
    na, nb = w_a.shape[1], w_b.shape[1]
    tiles_per_batch = seq // tm
    return pl.pallas_call(
        functools.partial(_inproj_kernel, mod=mod),
        grid=(t // tm,),
        in_specs=[pl.BlockSpec((tm, d), lambda i: (i, 0)),
                  pl.BlockSpec((None, N_MOD, d), lambda i: (i // tiles_per_batch, 0, 0)),
                  _resident((1, d)), _resident(w_a.shape), _resident(w_b.shape)],
        out_specs=[pl.BlockSpec((tm, na), lambda i: (i, 0)),
                   pl.BlockSpec((tm, nb), lambda i: (i, 0))],
        out_shape=[jax.ShapeDtypeStruct((t, na), BF16), jax.ShapeDtypeStruct((t, nb), F32)],
        compiler_params=_cparams(("parallel",)),
        name="inproj",
    )(x, ada3, g, w_a, w_b)


def _mla_prep_kernel(lat_ref, gq_ref, gkv_ref, wuq_ref, wukv_ref, cos_ref, sin_ref,
                     q_ref, k_ref, v_ref, *, scale):
    lat = lat_ref[...]
    qn = (_rms(lat[:, :MLA_Q_RANK]) * gq_ref[...]).astype(BF16)
    kvn = (_rms(lat[:, MLA_Q_RANK:MLA_Q_RANK + MLA_KV_RANK]) * gkv_ref[...]).astype(BF16)
    kpe = lat[:, MLA_Q_RANK + MLA_KV_RANK:MLA_Q_RANK + MLA_KV_RANK + LANE]
    q = jnp.dot(qn, wuq_ref[...], preferred_element_type=F32)
    kv = jnp.dot(kvn, wukv_ref[...], preferred_element_type=F32)
    cos = cos_ref[...]
    sin = sin_ref[...]
    kpe_r = (kpe * cos + pltpu.roll(kpe, LANE // 2, axis=1) * sin).astype(BF16)
    lane = lax.broadcasted_iota(jnp.int32, (1, LANE), 1)
    nope_w = MLA_HEADS * MLA_NOPE
    for p in range(MLA_HEADS // 2):
        pe = q[:, nope_w + LANE * p:nope_w + LANE * (p + 1)]
        pe_r = (pe * cos + pltpu.roll(pe, LANE // 2, axis=1) * sin) * scale
        for e in range(2):
            h = 2 * p + e
            keep = ((lane // 32) % 2) == e
            q_ref[h, :, 0:LANE] = (q[:, LANE * h:LANE * (h + 1)] * scale).astype(BF16)
            q_ref[h, :, LANE:2 * LANE] = jnp.where(keep, pe_r, 0.0).astype(BF16)
            k_ref[h, :, 0:LANE] = kv[:, LANE * h:LANE * (h + 1)].astype(BF16)
            k_ref[h, :, LANE:2 * LANE] = kpe_r
    v_ref[...] = kv[:, nope_w:].T.astype(BF16)


def _mla_prep(lat, gq, gkv, wuq, wukv, cos_a, sin_a, *, batch, seq, tm=512):
    t, nl = lat.shape
    tiles_per_batch = seq // tm
    scale = float((MLA_NOPE + MLA_ROPE) ** -0.5 * np.log2(np.e))
    qk_shape = jax.ShapeDtypeStruct((batch, MLA_HEADS, seq, QK_PAD), BF16)
    qk_spec = pl.BlockSpec((None, MLA_HEADS, tm, QK_PAD),
                           lambda i: (i // tiles_per_batch, 0, i % tiles_per_batch, 0))
    return pl.pallas_call(
        functools.partial(_mla_prep_kernel, scale=scale),
        grid=(t // tm,),
        in_specs=[pl.BlockSpec((tm, nl), lambda i: (i, 0)),
                  pl.BlockSpec((1, MLA_Q_RANK), lambda i: (0, 0)),
                  pl.BlockSpec((1, MLA_KV_RANK), lambda i: (0, 0)),
                  pl.BlockSpec(wuq.shape, lambda i: (0, 0)),
                  pl.BlockSpec(wukv.shape, lambda i: (0, 0)),
                  pl.BlockSpec((tm, LANE), lambda i: (i % tiles_per_batch, 0)),
                  pl.BlockSpec((tm, LANE), lambda i: (i % tiles_per_batch, 0))],
        out_specs=[qk_spec, qk_spec,
                   pl.BlockSpec((None, MLA_HEADS * MLA_V, tm),
                                lambda i: (i // tiles_per_batch, 0, i % tiles_per_batch))],
        out_shape=[qk_shape, qk_shape,
                   jax.ShapeDtypeStruct((batch, MLA_HEADS * MLA_V, seq), BF16)],
        compiler_params=_cparams(("parallel",)),
        name="mla_prep",
    )(lat, gq, gkv, wuq, wukv, cos_a, sin_a)


def _attn_kernel(q_ref, k_ref, vt_ref, o_ref, *scratch, tq, tk):
    qi = pl.program_id(2)
    assert tq == 2 * tk
    hp = q_ref.shape[0]
    heads = range(hp)
    s_ref, p_ref, alpha_ref = ([[scratch[(g * hp + e) * 2 + sl] for sl in range(2)] for e in heads]
                               for g in range(3))
    m_ref, l_ref, acc_ref = (scratch[6 * hp + g * hp:6 * hp + (g + 1) * hp] for g in range(3))
    full = slice(0, tq)
    upper = slice(tk, tq)

    def scores(i, slot, qs=full):
        start = pl.multiple_of(i * tk, tk)
        for e in heads:
            s_ref[e][slot][:, qs] = lax.dot_general(
                k_ref[e, pl.ds(start, tk), :], q_ref[e, qs, :], (((1,), (1,)), ((), ())),
                preferred_element_type=F32)

    def softmax(slot, qs=full, diagonal=False):
        for e in heads:
            s = s_ref[e][slot][:, qs]
            if diagonal:
                key = lax.broadcasted_iota(jnp.int32, (tk, 1), 0) // CHUNK
                qry = lax.broadcasted_iota(jnp.int32, (1, s.shape[1]), 1) // CHUNK
                s = jnp.where(key <= qry, s, NEG_BIG)
            m = m_ref[e][:, qs]
            m_new = jnp.maximum(m, jnp.max(s, axis=0, keepdims=True))
            p = jnp.exp2(s - m_new)
            alpha = jnp.exp2(m - m_new)
            l_ref[e][:, qs] = alpha * l_ref[e][:, qs] + jnp.sum(p, axis=0, keepdims=True)
            m_ref[e][:, qs] = m_new
            alpha_ref[e][slot][:, qs] = alpha
            p_ref[e][slot][:, qs] = p.astype(BF16)

    def pv(i, slot, qs=full):
        start = pl.multiple_of(i * tk, tk)
        for e in heads:
            acc_ref[e][:, qs] = alpha_ref[e][slot][:, qs] * acc_ref[e][:, qs] + jnp.dot(
                vt_ref[e * MLA_V:(e + 1) * MLA_V, pl.ds(start, tk)], p_ref[e][slot][:, qs],
                preferred_element_type=F32)

    for e in heads:
        m_ref[e][...] = jnp.full(m_ref[e].shape, NEG_BIG, F32)
        l_ref[e][...] = jnp.zeros(l_ref[e].shape, F32)
        acc_ref[e][...] = jnp.zeros(acc_ref[e].shape, F32)
        p_ref[e][1][...] = jnp.zeros(p_ref[e][1].shape, BF16)
        alpha_ref[e][1][...] = jnp.ones(alpha_ref[e][1].shape, F32)
    scores(0, 0)

    def body(j, _):
        a = 2 * j
        pv(jnp.maximum(a - 1, 0), 1)
        scores(a + 1, 1)
        softmax(0)
        pv(a, 0)
        scores(a + 2, 0)
        softmax(1)
        return 0

    lax.fori_loop(0, qi, body, 0)
    a = 2 * qi
    pv(jnp.maximum(a - 1, 0), 1)
    scores(a + 1, 1, upper)
    softmax(0, full, diagonal=True)
    pv(a, 0)
    softmax(1, upper, diagonal=True)
    pv(a + 1, 1, upper)
    for e in heads:
        o_ref[:, e * MLA_V:(e + 1) * MLA_V] = (
            acc_ref[e][...] * (1.0 / l_ref[e][...])).T.astype(o_ref.dtype)


def _attention(q, k, vt, *, batch, seq, tq=1024, hp=2):
    nq = seq // tq
    tk = tq // 2
    return pl.pallas_call(
        functools.partial(_attn_kernel, tq=tq, tk=tk),
        grid=(batch, MLA_HEADS // hp, nq),
        in_specs=[pl.BlockSpec((None, hp, tq, QK_PAD), lambda b, h, i: (b, h, i, 0)),
                  pl.BlockSpec((None, hp, seq, QK_PAD), lambda b, h, i: (b, h, 0, 0)),
                  pl.BlockSpec((None, hp * MLA_V, seq), lambda b, h, i: (b, h, 0))],
        out_specs=pl.BlockSpec((tq, hp * MLA_V), lambda b, h, i: (b * nq + i, h)),
        out_shape=jax.ShapeDtypeStruct((batch * seq, MLA_HEADS * MLA_V), BF16),
        scratch_shapes=([pltpu.VMEM((tk, tq), F32)] * (2 * hp)
                        + [pltpu.VMEM((tk, tq), BF16)] * (2 * hp)
                        + [pltpu.VMEM((1, tq), F32)] * (2 * hp)
                        + [pltpu.VMEM((1, tq), F32)] * hp
                        + [pltpu.VMEM((1, tq), F32)] * hp
                        + [pltpu.VMEM((MLA_V, tq), F32)] * hp),
        compiler_params=_cparams(("parallel", "parallel", "arbitrary")),
        name="attention",
    )(q, k, vt)


def _split3(x):
    hi = x.astype(BF16)
    r = x - hi.astype(F32)
    mid = r.astype(BF16)
    lo = (r - mid.astype(F32)).astype(BF16)
    return hi, mid, lo


def _gdn_kernel(qkv_ref, z_ref, ab_ref, wconv_ref, prm_ref, o_ref, xbuf, ybuf, state, *, nc):
    n = pl.program_id(1)
    c = CHUNK
    rows = nc * c
    hw = GDN_HEADS * GDN_DK
    heads = range(GDN_HEADS)

    @pl.when(n == 0)
    def _():
        xbuf[0:8, :] = jnp.zeros((8, xbuf.shape[1]), F32)
        state[...] = jnp.zeros_like(state)

    xbuf[8:8 + rows, :] = qkv_ref[...].astype(F32)
    w = wconv_ref[...]
    y = w[3:4, :] * xbuf[8:8 + rows, :]
    for j in range(1, CONV_W):
        y = y + w[3 - j:4 - j, :] * xbuf[8 - j:8 - j + rows, :]
    ybuf[...] = _silu(y)
    xbuf[0:8, :] = xbuf[rows:rows + 8, :]

    ri = lax.broadcasted_iota(jnp.int32, (c, c), 0)
    ci = lax.broadcasted_iota(jnp.int32, (c, c), 1)
    causal = ri >= ci
    strict = ri > ci
    tri = jnp.where(causal, 1.0, 0.0).astype(BF16)
    eye = jnp.where(ri == ci, 1.0, 0.0)
    g_out = prm_ref[2:3, :]
    a_neg = -jnp.exp(prm_ref[0:1, :])
    nt = (((1,), (1,)), ((), ()))

    units = [(k, h) for k in range(nc) for h in heads]
    dec = []
    for k in range(nc):
        ab = ab_ref[k * c:(k + 1) * c, :]
        xg = ab + prm_ref[1:2, :]
        softplus = jnp.maximum(xg, 0.0) + jnp.log1p(jnp.exp(-jnp.abs(xg)))
        g_hi, g_mid, g_lo = _split3(a_neg * softplus)
        dcol = (jnp.dot(tri, g_hi, preferred_element_type=F32)
                + jnp.dot(tri, g_mid, preferred_element_type=F32)
                + jnp.dot(tri, g_lo, preferred_element_type=F32))
        dlast = dcol[c - 1:c, :]
        dec.append(dict(dcol=dcol, drow=dcol.T, beta=_sigmoid(ab), e=jnp.exp(dcol),
                        kdf=jnp.exp(dlast - dcol), cd=jnp.exp(dlast)))

    st = {}
    for (k, h) in units:
        r0 = k * c
        d = dec[k]
        qh = ybuf[r0:r0 + c, h * GDN_DK:(h + 1) * GDN_DK]
        kh = ybuf[r0:r0 + c, hw + h * GDN_DK:hw + (h + 1) * GDN_DK]
        vh = ybuf[r0:r0 + c, 2 * hw + h * GDN_DV:2 * hw + (h + 1) * GDN_DV]
        qh = qh * (lax.rsqrt(jnp.sum(qh * qh, axis=-1, keepdims=True) + EPS) * (GDN_DK ** -0.5))
        kh = kh * lax.rsqrt(jnp.sum(kh * kh, axis=-1, keepdims=True) + EPS)
        beta = d["beta"][:, GDN_HEADS + h:GDN_HEADS + h + 1]
        ed = d["e"][:, h:h + 1]
        kb = kh * beta
        kh16 = kh.astype(BF16)
        kk = lax.dot_general(kb.astype(BF16), kh16, nt, preferred_element_type=F32)
        qk = lax.dot_general(qh.astype(BF16), kh16, nt, preferred_element_type=F32)
        st[k, h] = dict(
            kk=kk, qk=qk,
            rhs=jnp.concatenate([kb * ed, vh * beta], axis=1).astype(BF16),
            qe=(qh * ed).astype(BF16),
            kd_t=(kh * d["kdf"][:, h:h + 1]).T.astype(BF16))
    for (k, h) in units:
        d, s = dec[k], st[k, h]
        lmat = jnp.exp(jnp.where(causal, d["dcol"][:, h:h + 1] - d["drow"][h:h + 1, :], NEG_BIG))
        s["a"] = jnp.where(strict, s.pop("kk") * lmat, 0.0)
        s["attn"] = (s.pop("qk") * lmat).astype(BF16)

    def sub_mask(b):
        return ((ri // (2 * b)) == (ci // (2 * b))) & ((ri % (2 * b)) >= b) & ((ci % (2 * b)) < b)

    for u in units:
        st[u]["t"] = eye - jnp.where(sub_mask(1), st[u]["a"], 0.0)
    blk = 2
    while blk < c:
        m = sub_mask(blk)
        for u in units:
            s = st[u]
            s["x"] = jnp.dot(jnp.where(m, s["a"], 0.0).astype(BF16), s["t"].astype(BF16),
                             preferred_element_type=F32)
        for u in units:
            s = st[u]
            s["t"] = s["t"] - jnp.dot(s["t"].astype(BF16), s.pop("x").astype(BF16),
                                      preferred_element_type=F32)
        blk *= 2
    for u in units:
        s = st[u]
        s["wu"] = jnp.dot(s.pop("t").astype(BF16), s.pop("rhs"), preferred_element_type=F32)

    for k in range(nc):
        r0 = k * c
        ws_qs = {}
        for h in heads:
            s = st[k, h]
            lhs = jnp.concatenate([s["wu"][:, :GDN_DK].astype(BF16), s["qe"]], axis=0)
            ws_qs[h] = jnp.dot(lhs, state[h].astype(BF16), preferred_element_type=F32)
        for h in heads:
            s = st[k, h]
            sl = slice(h * GDN_DV, (h + 1) * GDN_DV)
            v16 = (s["wu"][:, GDN_DK:] - ws_qs[h][:c, :]).astype(BF16)
            o = ws_qs[h][c:, :] + jnp.dot(s["attn"], v16, preferred_element_type=F32)
            state[h] = (state[h] * dec[k]["cd"][:, h:h + 1]
                        + jnp.dot(s["kd_t"], v16, preferred_element_type=F32))
            zh = z_ref[r0:r0 + c, sl].astype(F32)
            o_ref[r0:r0 + c, sl] = ((_rms(o) * g_out) * _silu(zh)).astype(o_ref.dtype)


def _gdn(pa, lat, wconv, prm, *, batch, seq, nc=4):
    rows = nc * CHUNK
    nstep = seq // rows
    hw3 = 3 * GDN_HEADS * GDN_DK
    hv = GDN_HEADS * GDN_DV
    ab_blk = (MLA_Q_RANK + MLA_KV_RANK + LANE) // LANE
    z_blk = pa.shape[1] // hv - 1
    return pl.pallas_call(
        functools.partial(_gdn_kernel, nc=nc),
        grid=(batch, nstep),
        in_specs=[pl.BlockSpec((rows, hw3), lambda b, n: (b * nstep + n, 0)),
                  pl.BlockSpec((rows, hv), lambda b, n: (b * nstep + n, z_blk)),
                  pl.BlockSpec((rows, LANE), lambda b, n: (b * nstep + n, ab_blk)),
                  pl.BlockSpec(wconv.shape, lambda b, n: (0, 0)),
                  pl.BlockSpec(prm.shape, lambda b, n: (0, 0))],
        out_specs=pl.BlockSpec((rows, hv), lambda b, n: (b * nstep + n, 0)),
        out_shape=jax.ShapeDtypeStruct((batch * seq, hv), BF16),
        scratch_shapes=[pltpu.VMEM((rows + 8, hw3), F32),
                        pltpu.VMEM((rows, hw3), F32),
                        pltpu.VMEM((GDN_HEADS, GDN_DK, GDN_DV), F32)],
        compiler_params=_cparams(("parallel", "arbitrary")),
        name="gdn",
    )(pa, pa, lat, wconv, prm)


def _merge_kernel(om_ref, og_ref, gm_ref, gg_ref, x_ref, ada_ref, wm_ref, wg_ref, wo_ref, o_ref):
    ym = jnp.dot(om_ref[...], wm_ref[...], preferred_element_type=F32)
    yg = jnp.dot(og_ref[...], wg_ref[...], preferred_element_type=F32)
    mix = _sigmoid(gm_ref[...].astype(F32)) * ym + _sigmoid(gg_ref[...].astype(F32)) * yg
    mixed = jnp.dot(mix.astype(BF16), wo_ref[...], preferred_element_type=F32)
    o_ref[...] = x_ref[...] + ada_ref[5:6, :] * mixed


def _merge(o_mla, o_gdn, pa, x1, ada3, wm, wg, wo, *, seq, tm=512):
    t, d = x1.shape
    tiles_per_batch = seq // tm
    gate_blk = 3 * GDN_HEADS * GDN_DK // d
    wspec = _resident((d, d))
    return pl.pallas_call(
        _merge_kernel,
        grid=(t // tm,),
        in_specs=[pl.BlockSpec((tm, d), lambda i: (i, 0)),
                  pl.BlockSpec((tm, d), lambda i: (i, 0)),
                  pl.BlockSpec((tm, d), lambda i: (i, gate_blk)),
                  pl.BlockSpec((tm, d), lambda i: (i, gate_blk + 1)),
                  pl.BlockSpec((tm, d), lambda i: (i, 0)),
                  pl.BlockSpec((None, N_MOD, d), lambda i: (i // tiles_per_batch, 0, 0)),
                  wspec, wspec, wspec],
        out_specs=pl.BlockSpec((tm, d), lambda i: (i, 0)),
        out_shape=jax.ShapeDtypeStruct((t, d), F32),
        compiler_params=_cparams(("parallel",)),
        name="merge",
    )(o_mla, o_gdn, pa, pa, x1, ada3, wm, wg, wo)


def _uq_perm():
    hd = MLA_NOPE + MLA_ROPE
    half = MLA_ROPE // 2
    cols = [h * hd + d for h in range(MLA_HEADS) for d in range(MLA_NOPE)]
    for p in range(MLA_HEADS // 2):
        for part in range(2):
            for e in range(2):
                h = 2 * p + e
                cols += [h * hd + MLA_NOPE + part * half + r for r in range(half)]
    return np.asarray(cols, np.int32)


def _ukv_perm():
    hd = MLA_NOPE + MLA_V
    k = [h * hd + d for h in range(MLA_HEADS) for d in range(MLA_NOPE)]
    v = [h * hd + MLA_NOPE + d for h in range(MLA_HEADS) for d in range(MLA_V)]
    return np.asarray(k + v, np.int32)


def kernel(x, c, w_ada, b_ada, g_ffn1, w1_gate, w1_up, w1_down, g_mix, w_in, g_q_lat, w_uq, g_kv_lat, w_ukv, w_conv, a_log, dt_bias, g_gdn_out, w_o_mla, w_o_gdn, w_out, g_ffn2, w2_gate, w2_up, w2_down, g_final):
    batch, seq, d = x.shape
    depth = w_ada.shape[0]
    t = batch * seq
    half = MLA_ROPE // 2

    pos = jnp.arange(seq, dtype=F32)
    inv_freq = ROPE_THETA ** (-jnp.arange(0, MLA_ROPE, 2, dtype=F32) / MLA_ROPE)
    ang = pos[:, None] * inv_freq[None, :]
    cos, sin = jnp.cos(ang), jnp.sin(ang)
    cos_a = jnp.concatenate([cos, cos, cos, cos], axis=1)
    sin_a = jnp.concatenate([-sin, -sin, sin, sin], axis=1)

    o_q, o_kv, o_pe = 0, MLA_Q_RANK, MLA_Q_RANK + MLA_KV_RANK
    o_gq = o_pe + MLA_ROPE
    hw = GDN_HEADS * GDN_DK
    o_ga = o_gq + 3 * hw
    o_gb = o_ga + GDN_HEADS
    o_gz = o_gb + GDN_HEADS
    o_gates = o_gz + GDN_HEADS * GDN_DV

    c_pad = jnp.zeros((8, d), F32).at[:batch].set(c)
    xf = x.reshape(t, d)
    uq_perm = _uq_perm()
    ukv_perm = _ukv_perm()

    for l in range(depth):
        ada = _ada(c_pad, w_ada[l], b_ada[l][None, :])
        ada3 = ada[:batch].reshape(batch, N_MOD, d)

        xf = _ffn(xf, ada3, g_ffn1[l][None, :], w1_gate[l].astype(BF16), w1_up[l].astype(BF16),
                  w1_down[l].astype(BF16), g_final[None, :], mod=0, seq=seq, final_norm=False)

        wi = w_in[l]
        w_a = jnp.concatenate([wi[:, o_gq:o_ga], wi[:, o_gates:], wi[:, o_gz:o_gates]],
                              axis=1).astype(BF16)
        pe1 = wi[:, o_pe:o_pe + half]
        pe2 = wi[:, o_pe + half:o_gq]
        w_b = jnp.concatenate([wi[:, o_q:o_pe], pe1, pe1, pe2, pe2, wi[:, o_ga:o_gz],
                               jnp.zeros((d, LANE - 2 * GDN_HEADS), F32)], axis=1).astype(BF16)
        g_mix_l = g_mix[l][None, :]
        pa, lat = _inproj(xf, ada3, g_mix_l, w_a, w_b, mod=1, seq=seq)

        q, k, vt = _mla_prep(lat, g_q_lat[l][None, :], g_kv_lat[l][None, :],
                            w_uq[l][:, uq_perm].astype(BF16), w_ukv[l][:, ukv_perm].astype(BF16),
                            cos_a, sin_a, batch=batch, seq=seq)
        o_mla = _attention(q, k, vt, batch=batch, seq=seq)

        prm = jnp.zeros((8, LANE), F32)
        prm = prm.at[0, :GDN_HEADS].set(a_log[l]).at[1, :GDN_HEADS].set(dt_bias[l])
        prm = prm.at[2, :].set(g_gdn_out[l])
        o_gdn = _gdn(pa, lat, w_conv[l], prm, batch=batch, seq=seq)

        xf = _merge(o_mla, o_gdn, pa, xf, ada3, w_o_mla[l].astype(BF16), w_o_gdn[l].astype(BF16),
                    w_out[l].astype(BF16), seq=seq)

        xf = _ffn(xf, ada3, g_ffn2[l][None, :], w2_gate[l].astype(BF16), w2_up[l].astype(BF16),
                  w2_down[l].astype(BF16), g_final[None, :], mod=2, seq=seq,
                  final_norm=(l == depth - 1))
    return xf.reshape(batch, seq, d)
```

```python
import functools

import numpy as np
import jax
import jax.numpy as jnp
from jax import lax
from jax.experimental import pallas as pl
from jax.experimental.pallas import tpu as pltpu

F32 = jnp.float32
BF16 = jnp.bfloat16

EPS = 1e-6
D_MODEL = 1024
D_FF = 2816
N_MOD = 9
CHUNK = 64
MLA_HEADS = 8
MLA_Q_RANK = 768
MLA_KV_RANK = 256
MLA_NOPE = 128
MLA_ROPE = 64
MLA_V = 128
ROPE_THETA = 10000.0
GDN_HEADS = 8
GDN_DK = 128
GDN_DV = 128
CONV_W = 4
LANE = 128
QK_PAD = 256
NEG_BIG = -1e30
VMEM_LIMIT = 56 * 1024 * 1024


def _cparams(sem):
    return pltpu.CompilerParams(dimension_semantics=sem, vmem_limit_bytes=VMEM_LIMIT)


def _sigmoid(x):
    return 1.0 / (1.0 + jnp.exp(-x))


def _silu(x):
    return x * _sigmoid(x)


def _rms(x):
    return x * lax.rsqrt(jnp.mean(x * x, axis=-1, keepdims=True) + EPS)


def _modulate(x, g, shift, scale):
    return (_rms(x) * g) * (1.0 + scale) + shift


def _ada_kernel(c_ref, w_ref, b_ref, o_ref):
    c = _silu(c_ref[...]).astype(BF16)
    o_ref[...] = jnp.dot(c, w_ref[...].astype(BF16), preferred_element_type=F32) + b_ref[...]


def _ada(c_pad, w_ada, b_ada):
    m, d = c_pad.shape
    n = w_ada.shape[1]
    tn = 1024
    return pl.pallas_call(
        _ada_kernel,
        grid=(n // tn,),
        in_specs=[pl.BlockSpec((m, d), lambda j: (0, 0)),
                  pl.BlockSpec((d, tn), lambda j: (0, j)),
                  pl.BlockSpec((1, tn), lambda j: (0, j))],
        out_specs=pl.BlockSpec((m, tn), lambda j: (0, j)),
        out_shape=jax.ShapeDtypeStruct((m, n), F32),
        compiler_params=_cparams(("parallel",)),
        name="ada",
    )(c_pad, w_ada, b_ada)


def _ffn_kernel(x_ref, ada_ref, g_ref, wg_ref, wu_ref, wd_ref, gf_ref, o_ref, *, mod, final_norm):
    x = x_ref[...]
    h = _modulate(x, g_ref[...], ada_ref[3 * mod:3 * mod + 1, :],
                  ada_ref[3 * mod + 1:3 * mod + 2, :]).astype(BF16)
    a = jnp.dot(h, wg_ref[...], preferred_element_type=F32)
    b = jnp.dot(h, wu_ref[...], preferred_element_type=F32)
    act = (_silu(a) * b).astype(BF16)
    gt = ada_ref[3 * mod + 2:3 * mod + 3, :]
    y = x + (0.5 * gt) * jnp.dot(act, wd_ref[...], preferred_element_type=F32)
    if final_norm:
        y = _rms(y) * gf_ref[...]
    o_ref[...] = y


def _resident(shape):
    return pl.BlockSpec(shape, lambda *_: (0,) * len(shape), pipeline_mode=pl.Buffered(1))


def _ffn(x, ada3, g, wg, wu, wd, gf, *, mod, seq, final_norm, tm=512):
    t, d = x.shape
    tiles_per_batch = seq // tm
    kern = functools.partial(_ffn_kernel, mod=mod, final_norm=final_norm)
    return pl.pallas_call(
        kern,
        grid=(t // tm,),
        in_specs=[pl.BlockSpec((tm, d), lambda i: (i, 0)),
                  pl.BlockSpec((None, N_MOD, d), lambda i: (i // tiles_per_batch, 0, 0)),
                  _resident((1, d)),
                  _resident(wg.shape), _resident(wu.shape), _resident(wd.shape),
                  _resident((1, d))],
        out_specs=pl.BlockSpec((tm, d), lambda i: (i, 0)),
        out_shape=jax.ShapeDtypeStruct((t, d), F32),
        compiler_params=_cparams(("parallel",)),
        name="ffn%d" % mod,
    )(x, ada3, g, wg, wu, wd, gf)


def _inproj_kernel(x_ref, ada_ref, g_ref, wa_ref, wb_ref, pa_ref, lat_ref, *, mod):
    h = _modulate(x_ref[...], g_ref[...], ada_ref[3 * mod:3 * mod + 1, :],
                  ada_ref[3 * mod + 1:3 * mod + 2, :]).astype(BF16)
    pa_ref[...] = jnp.dot(h, wa_ref[...], preferred_element_type=F32).astype(pa_ref.dtype)
    lat_ref[...] = jnp.dot(h, wb_ref[...], preferred_element_type=F32)


def _inproj(x, ada3, g, w_a, w_b, *, mod, seq, tm=512):
    t, d = x.shape
    na, nb = w_a.shape[1], w_b.shape[1]
    tiles_per_batch = seq // tm
    return pl.pallas_call(
        functools.partial(_inproj_kernel, mod=mod),
        grid=(t // tm,),
        in_specs=[pl.BlockSpec((tm, d), lambda i: (i, 0)),
                  pl.BlockSpec((None, N_MOD, d), lambda i: (i // tiles_per_batch, 0, 0)),
                  _resident((1, d)), _resident(w_a.shape), _resident(w_b.shape)],
        out_specs=[pl.BlockSpec((tm, na), lambda i: (i, 0)),
                   pl.BlockSpec((tm, nb), lambda i: (i, 0))],
        out_shape=[jax.ShapeDtypeStruct((t, na), BF16), jax.ShapeDtypeStruct((t, nb), F32)],
        compiler_params=_cparams(("parallel",)),
        name="inproj",
    )(x, ada3, g, w_a, w_b)


def _mla_prep_kernel(lat_ref, gq_ref, gkv_ref, wuq_ref, wukv_ref, cos_ref, sin_ref,
                     q_ref, k_ref, v_ref, *, scale):
    lat = lat_ref[...]
    qn = (_rms(lat[:, :MLA_Q_RANK]) * gq_ref[...]).astype(BF16)
    kvn = (_rms(lat[:, MLA_Q_RANK:MLA_Q_RANK + MLA_KV_RANK]) * gkv_ref[...]).astype(BF16)
    kpe = lat[:, MLA_Q_RANK + MLA_KV_RANK:MLA_Q_RANK + MLA_KV_RANK + LANE]
    q = jnp.dot(qn, wuq_ref[...], preferred_element_type=F32)
    kv = jnp.dot(kvn, wukv_ref[...], preferred_element_type=F32)
    cos = cos_ref[...]
    sin = sin_ref[...]
    kpe_r = (kpe * cos + pltpu.roll(kpe, LANE // 2, axis=1) * sin).astype(BF16)
    lane = lax.broadcasted_iota(jnp.int32, (1, LANE), 1)
    nope_w = MLA_HEADS * MLA_NOPE
    for p in range(MLA_HEADS // 2):
        pe = q[:, nope_w + LANE * p:nope_w + LANE * (p + 1)]
        pe_r = (pe * cos + pltpu.roll(pe, LANE // 2, axis=1) * sin) * scale
        for e in range(2):
            h = 2 * p + e
            keep = ((lane // 32) % 2) == e
            q_ref[h, :, 0:LANE] = (q[:, LANE * h:LANE * (h + 1)] * scale).astype(BF16)
            q_ref[h, :, LANE:2 * LANE] = jnp.where(keep, pe_r, 0.0).astype(BF16)
            k_ref[h, :, 0:LANE] = kv[:, LANE * h:LANE * (h + 1)].astype(BF16)
            k_ref[h, :, LANE:2 * LANE] = kpe_r
    v_ref[...] = kv[:, nope_w:].T.astype(BF16)


def _mla_prep(lat, gq, gkv, wuq, wukv, cos_a, sin_a, *, batch, seq, tm=512):
    t, nl = lat.shape
    tiles_per_batch = seq // tm
    scale = float((MLA_NOPE + MLA_ROPE) ** -0.5 * np.log2(np.e))
    qk_shape = jax.ShapeDtypeStruct((batch, MLA_HEADS, seq, QK_PAD), BF16)
    qk_spec = pl.BlockSpec((None, MLA_HEADS, tm, QK_PAD),
                           lambda i: (i // tiles_per_batch, 0, i % tiles_per_batch, 0))
    return pl.pallas_call(
        functools.partial(_mla_prep_kernel, scale=scale),
        grid=(t // tm,),
        in_specs=[pl.BlockSpec((tm, nl), lambda i: (i, 0)),
                  pl.BlockSpec((1, MLA_Q_RANK), lambda i: (0, 0)),
                  pl.BlockSpec((1, MLA_KV_RANK), lambda i: (0, 0)),
                  pl.BlockSpec(wuq.shape, lambda i: (0, 0)),
                  pl.BlockSpec(wukv.shape, lambda i: (0, 0)),
                  pl.BlockSpec((tm, LANE), lambda i: (i % tiles_per_batch, 0)),
                  pl.BlockSpec((tm, LANE), lambda i: (i % tiles_per_batch, 0))],
        out_specs=[qk_spec, qk_spec,
                   pl.BlockSpec((None, MLA_HEADS * MLA_V, tm),
                                lambda i: (i // tiles_per_batch, 0, i % tiles_per_batch))],
        out_shape=[qk_shape, qk_shape,
                   jax.ShapeDtypeStruct((batch, MLA_HEADS * MLA_V, seq), BF16)],
        compiler_params=_cparams(("parallel",)),
        name="mla_prep",
    )(lat, gq, gkv, wuq, wukv, cos_a, sin_a)


def _attn_seq_kernel(q_ref, k_ref, vt_ref, o_ref, *scratch, tq, tk):
    assert tq == 2 * tk
    hp, seq = q_ref.shape[0], q_ref.shape[1]
    heads = range(hp)
    buf = iter(scratch)
    s_ref, p_ref, alpha_ref, m_ref, l_ref, acc_ref = (
        [[next(buf) for _ in range(2)] for _ in heads] for _ in range(6))
    cw = 256
    blocks = [slice(c0, c0 + cw) for c0 in range(0, tq, cw)]
    upper = [qs for qs in blocks if qs.start >= tk]

    tasks = []
    starts, ends = set(), set()
    for t in range(seq // tq):
        starts.add(len(tasks))
        tasks += [(t, kt, blocks, None) for kt in range(2 * t)]
        tasks += [(t, 2 * t, blocks, 0), (t, 2 * t + 1, upper, tk)]
        ends.add(len(tasks) - 1)

    def scores(x):
        t, kt, _, _ = tasks[x]

        def run(qs):
            for e in heads:
                s_ref[e][x % 2][:, qs] = lax.dot_general(
                    k_ref[e, kt * tk:(kt + 1) * tk, :], q_ref[e, t * tq + qs.start:t * tq + qs.stop, :],
                    (((1,), (1,)), ((), ())), preferred_element_type=F32)
        return run

    def softmax(x):
        t, _, _, key_off = tasks[x]
        slot, par = x % 2, t % 2

        def run(qs):
            for e in heads:
                s = s_ref[e][slot][:, qs]
                if key_off is not None and key_off + tk > qs.start:
                    key = (lax.broadcasted_iota(jnp.int32, (tk, 1), 0) + key_off) // CHUNK
                    qry = (lax.broadcasted_iota(jnp.int32, (1, cw), 1) + qs.start) // CHUNK
                    s = jnp.where(key <= qry, s, NEG_BIG)
                m = m_ref[e][par][:, qs]
                m_new = jnp.maximum(m, jnp.max(s, axis=0, keepdims=True))
                p = jnp.exp2(s - m_new)
                alpha = jnp.exp2(m - m_new)
                l_ref[e][par][:, qs] = alpha * l_ref[e][par][:, qs] + jnp.sum(p, axis=0, keepdims=True)
                m_ref[e][par][:, qs] = m_new
                alpha_ref[e][slot][:, qs] = alpha
                p_ref[e][slot][:, qs] = p.astype(BF16)
        return run

    def pv(x):
        t, kt, _, _ = tasks[x]
        slot, par = x % 2, t % 2

        def run(qs):
            for e in heads:
                acc_ref[e][par][:, qs] = (
                    alpha_ref[e][slot][:, qs] * acc_ref[e][par][:, qs]
                    + jnp.dot(vt_ref[e * MLA_V:(e + 1) * MLA_V, kt * tk:(kt + 1) * tk],
                              p_ref[e][slot][:, qs], preferred_element_type=F32))
        return run

    def interleave(stages):
        for i in range(max(len(b) for _, b in stages)):
            for fn, b in stages:
                if i < len(b):
                    fn(b[i])

    n = len(tasks)
    for x in range(n + 2):
        stages = []
        if x >= 2:
            stages.append((pv(x - 2), tasks[x - 2][2]))
        if x < n:
            stages.append((scores(x), tasks[x][2]))
        if 1 <= x <= n:
            if x - 1 in starts:
                for e in heads:
                    par = tasks[x - 1][0] % 2
                    m_ref[e][par][...] = jnp.full((1, tq), NEG_BIG, F32)
                    l_ref[e][par][...] = jnp.zeros((1, tq), F32)
                    acc_ref[e][par][...] = jnp.zeros((MLA_V, tq), F32)
            stages.append((softmax(x - 1), tasks[x - 1][2]))
        interleave(stages)
        if x - 2 in ends:
            t = tasks[x - 2][0]
            for e in heads:
                o_ref[t * tq:(t + 1) * tq, e * MLA_V:(e + 1) * MLA_V] = (
                    acc_ref[e][t % 2][...] * (1.0 / l_ref[e][t % 2][...])).T.astype(o_ref.dtype)


def _attention(q, k, vt, *, batch, seq, tq=1024, hp=1):
    tk = tq // 2
    return pl.pallas_call(
        functools.partial(_attn_seq_kernel, tq=tq, tk=tk),
        grid=(batch, MLA_HEADS // hp),
        in_specs=[pl.BlockSpec((None, hp, seq, QK_PAD), lambda b, h: (b, h, 0, 0)),
                  pl.BlockSpec((None, hp, seq, QK_PAD), lambda b, h: (b, h, 0, 0)),
                  pl.BlockSpec((None, hp * MLA_V, seq), lambda b, h: (b, h, 0))],
        out_specs=pl.BlockSpec((seq, hp * MLA_V), lambda b, h: (b, h)),
        out_shape=jax.ShapeDtypeStruct((batch * seq, MLA_HEADS * MLA_V), BF16),
        scratch_shapes=([pltpu.VMEM((tk, tq), F32)] * (2 * hp)
                        + [pltpu.VMEM((tk, tq), BF16)] * (2 * hp)
                        + [pltpu.VMEM((1, tq), F32)] * (2 * hp)
                        + [pltpu.VMEM((1, tq), F32)] * (2 * hp)
                        + [pltpu.VMEM((1, tq), F32)] * (2 * hp)
                        + [pltpu.VMEM((MLA_V, tq), F32)] * (2 * hp)),
        compiler_params=_cparams(("parallel", "parallel")),
        name="attention",
    )(q, k, vt)


def _split3(x):
    hi = x.astype(BF16)
    r = x - hi.astype(F32)
    mid = r.astype(BF16)
    lo = (r - mid.astype(F32)).astype(BF16)
    return hi, mid, lo


def _gdn_kernel(qkv_ref, z_ref, ab_ref, wconv_ref, prm_ref, o_ref, xbuf, ybuf, state, *, nc):
    n = pl.program_id(1)
    c = CHUNK
    rows = nc * c
    hw = GDN_HEADS * GDN_DK
    heads = range(GDN_HEADS)

    @pl.when(n == 0)
    def _():
        xbuf[0:8, :] = jnp.zeros((8, xbuf.shape[1]), F32)
        state[...] = jnp.zeros_like(state)

    xbuf[8:8 + rows, :] = qkv_ref[...].astype(F32)
    w = wconv_ref[...]
    y = w[3:4, :] * xbuf[8:8 + rows, :]
    for j in range(1, CONV_W):
        y = y + w[3 - j:4 - j, :] * xbuf[8 - j:8 - j + rows, :]
    ybuf[...] = _silu(y)
    xbuf[0:8, :] = xbuf[rows:rows + 8, :]

    ri = lax.broadcasted_iota(jnp.int32, (c, c), 0)
    ci = lax.broadcasted_iota(jnp.int32, (c, c), 1)
    causal = ri >= ci
    strict = ri > ci
    tri = jnp.where(causal, 1.0, 0.0).astype(BF16)
    eye = jnp.where(ri == ci, 1.0, 0.0)
    g_out = prm_ref[2:3, :]
    a_neg = -jnp.exp(prm_ref[0:1, :])
    nt = (((1,), (1,)), ((), ()))

    units = [(k, h) for k in range(nc) for h in heads]
    dec = []
    for k in range(nc):
        ab = ab_ref[k * c:(k + 1) * c, :]
        xg = ab + prm_ref[1:2, :]
        softplus = jnp.maximum(xg, 0.0) + jnp.log1p(jnp.exp(-jnp.abs(xg)))
        g_hi, g_mid, g_lo = _split3(a_neg * softplus)
        dcol = (jnp.dot(tri, g_hi, preferred_element_type=F32)
                + jnp.dot(tri, g_mid, preferred_element_type=F32)
                + jnp.dot(tri, g_lo, preferred_element_type=F32))
        dlast = dcol[c - 1:c, :]
        dec.append(dict(dcol=dcol, drow=dcol.T, beta=_sigmoid(ab), e=jnp.exp(dcol),
                        kdf=jnp.exp(dlast - dcol), cd=jnp.exp(dlast)))

    st = {}
    for (k, h) in units:
        r0 = k * c
        d = dec[k]
        qh = ybuf[r0:r0 + c, h * GDN_DK:(h + 1) * GDN_DK]
        kh = ybuf[r0:r0 + c, hw + h * GDN_DK:hw + (h + 1) * GDN_DK]
        vh = ybuf[r0:r0 + c, 2 * hw + h * GDN_DV:2 * hw + (h + 1) * GDN_DV]
        qh = qh * (lax.rsqrt(jnp.sum(qh * qh, axis=-1, keepdims=True) + EPS) * (GDN_DK ** -0.5))
        kh = kh * lax.rsqrt(jnp.sum(kh * kh, axis=-1, keepdims=True) + EPS)
        beta = d["beta"][:, GDN_HEADS + h:GDN_HEADS + h + 1]
        ed = d["e"][:, h:h + 1]
        kb = kh * beta
        kh16 = kh.astype(BF16)
        kk = lax.dot_general(kb.astype(BF16), kh16, nt, preferred_element_type=F32)
        qk = lax.dot_general(qh.astype(BF16), kh16, nt, preferred_element_type=F32)
        st[k, h] = dict(
            kk=kk, qk=qk,
            rhs=jnp.concatenate([kb * ed, vh * beta], axis=1).astype(BF16),
            qe=(qh * ed).astype(BF16),
            kd_t=(kh * d["kdf"][:, h:h + 1]).T.astype(BF16))
    for (k, h) in units:
        d, s = dec[k], st[k, h]
        lmat = jnp.exp(jnp.where(causal, d["dcol"][:, h:h + 1] - d["drow"][h:h + 1, :], NEG_BIG))
        s["a"] = jnp.where(strict, s.pop("kk") * lmat, 0.0)
        s["attn"] = (s.pop("qk") * lmat).astype(BF16)

    def sub_mask(b):
        return ((ri // (2 * b)) == (ci // (2 * b))) & ((ri % (2 * b)) >= b) & ((ci % (2 * b)) < b)

    for u in units:
        st[u]["t"] = eye - jnp.where(sub_mask(1), st[u]["a"], 0.0)
    blk = 2
    while blk < c:
        m = sub_mask(blk)
        for u in units:
            s = st[u]
            s["x"] = jnp.dot(jnp.where(m, s["a"], 0.0).astype(BF16), s["t"].astype(BF16),
                             preferred_element_type=F32)
        for u in units:
            s = st[u]
            s["t"] = s["t"] - jnp.dot(s["t"].astype(BF16), s.pop("x").astype(BF16),
                                      preferred_element_type=F32)
        blk *= 2
    for u in units:
        s = st[u]
        s["wu"] = jnp.dot(s.pop("t").astype(BF16), s.pop("rhs"), preferred_element_type=F32)

    for k in range(nc):
        r0 = k * c
        ws_qs = {}
        for h in heads:
            s = st[k, h]
            lhs = jnp.concatenate([s["wu"][:, :GDN_DK].astype(BF16), s["qe"]], axis=0)
            ws_qs[h] = jnp.dot(lhs, state[h].astype(BF16), preferred_element_type=F32)
        for h in heads:
            s = st[k, h]
            sl = slice(h * GDN_DV, (h + 1) * GDN_DV)
            v16 = (s["wu"][:, GDN_DK:] - ws_qs[h][:c, :]).astype(BF16)
            o = ws_qs[h][c:, :] + jnp.dot(s["attn"], v16, preferred_element_type=F32)
            state[h] = (state[h] * dec[k]["cd"][:, h:h + 1]
                        + jnp.dot(s["kd_t"], v16, preferred_element_type=F32))
            zh = z_ref[r0:r0 + c, sl].astype(F32)
            o_ref[r0:r0 + c, sl] = ((_rms(o) * g_out) * _silu(zh)).astype(o_ref.dtype)


def _gdn(pa, lat, wconv, prm, *, batch, seq, nc=4):
    rows = nc * CHUNK
    nstep = seq // rows
    hw3 = 3 * GDN_HEADS * GDN_DK
    hv = GDN_HEADS * GDN_DV
    ab_blk = (MLA_Q_RANK + MLA_KV_RANK + LANE) // LANE
    z_blk = pa.shape[1] // hv - 1
    return pl.pallas_call(
        functools.partial(_gdn_kernel, nc=nc),
        grid=(batch, nstep),
        in_specs=[pl.BlockSpec((rows, hw3), lambda b, n: (b * nstep + n, 0)),
                  pl.BlockSpec((rows, hv), lambda b, n: (b * nstep + n, z_blk)),
                  pl.BlockSpec((rows, LANE), lambda b, n: (b * nstep + n, ab_blk)),
                  pl.BlockSpec(wconv.shape, lambda b, n: (0, 0)),
                  pl.BlockSpec(prm.shape, lambda b, n: (0, 0))],
        out_specs=pl.BlockSpec((rows, hv), lambda b, n: (b * nstep + n, 0)),
        out_shape=jax.ShapeDtypeStruct((batch * seq, hv), BF16),
        scratch_shapes=[pltpu.VMEM((rows + 8, hw3), F32),
                        pltpu.VMEM((rows, hw3), F32),
                        pltpu.VMEM((GDN_HEADS, GDN_DK, GDN_DV), F32)],
        compiler_params=_cparams(("parallel", "arbitrary")),
        name="gdn",
    )(pa, pa, lat, wconv, prm)


def _merge_kernel(om_ref, og_ref, gm_ref, gg_ref, x_ref, ada_ref, wm_ref, wg_ref, wo_ref, o_ref):
    ym = jnp.dot(om_ref[...], wm_ref[...], preferred_element_type=F32)
    yg = jnp.dot(og_ref[...], wg_ref[...], preferred_element_type=F32)
    mix = _sigmoid(gm_ref[...].astype(F32)) * ym + _sigmoid(gg_ref[...].astype(F32)) * yg
    mixed = jnp.dot(mix.astype(BF16), wo_ref[...], preferred_element_type=F32)
    o_ref[...] = x_ref[...] + ada_ref[5:6, :] * mixed


def _merge(o_mla, o_gdn, pa, x1, ada3, wm, wg, wo, *, seq, tm=512):
    t, d = x1.shape
    tiles_per_batch = seq // tm
    gate_blk = 3 * GDN_HEADS * GDN_DK // d
    wspec = _resident((d, d))
    return pl.pallas_call(
        _merge_kernel,
        grid=(t // tm,),
        in_specs=[pl.BlockSpec((tm, d), lambda i: (i, 0)),
                  pl.BlockSpec((tm, d), lambda i: (i, 0)),
                  pl.BlockSpec((tm, d), lambda i: (i, gate_blk)),
                  pl.BlockSpec((tm, d), lambda i: (i, gate_blk + 1)),
                  pl.BlockSpec((tm, d), lambda i: (i, 0)),
                  pl.BlockSpec((None, N_MOD, d), lambda i: (i // tiles_per_batch, 0, 0)),
                  wspec, wspec, wspec],
        out_specs=pl.BlockSpec((tm, d), lambda i: (i, 0)),
        out_shape=jax.ShapeDtypeStruct((t, d), F32),
        compiler_params=_cparams(("parallel",)),
        name="merge",
    )(o_mla, o_gdn, pa, pa, x1, ada3, wm, wg, wo)


def _uq_perm():
    hd = MLA_NOPE + MLA_ROPE
    half = MLA_ROPE // 2
    cols = [h * hd + d for h in range(MLA_HEADS) for d in range(MLA_NOPE)]
    for p in range(MLA_HEADS // 2):
        for part in range(2):
            for e in range(2):
                h = 2 * p + e
                cols += [h * hd + MLA_NOPE + part * half + r for r in range(half)]
    return np.asarray(cols, np.int32)


def _ukv_perm():
    hd = MLA_NOPE + MLA_V
    k = [h * hd + d for h in range(MLA_HEADS) for d in range(MLA_NOPE)]
    v = [h * hd + MLA_NOPE + d for h in range(MLA_HEADS) for d in range(MLA_V)]
    return np.asarray(k + v, np.int32)


def kernel(x, c, w_ada, b_ada, g_ffn1, w1_gate, w1_up, w1_down, g_mix, w_in, g_q_lat, w_uq, g_kv_lat, w_ukv, w_conv, a_log, dt_bias, g_gdn_out, w_o_mla, w_o_gdn, w_out, g_ffn2, w2_gate, w2_up, w2_down, g_final):
    batch, seq, d = x.shape
    depth = w_ada.shape[0]
    t = batch * seq
    half = MLA_ROPE // 2

    pos = jnp.arange(seq, dtype=F32)
    inv_freq = ROPE_THETA ** (-jnp.arange(0, MLA_ROPE, 2, dtype=F32) / MLA_ROPE)
    ang = pos[:, None] * inv_freq[None, :]
    cos, sin = jnp.cos(ang), jnp.sin(ang)
    cos_a = jnp.concatenate([cos, cos, cos, cos], axis=1)
    sin_a = jnp.concatenate([-sin, -sin, sin, sin], axis=1)

    o_q, o_kv, o_pe = 0, MLA_Q_RANK, MLA_Q_RANK + MLA_KV_RANK
    o_gq = o_pe + MLA_ROPE
    hw = GDN_HEADS * GDN_DK
    o_ga = o_gq + 3 * hw
    o_gb = o_ga + GDN_HEADS
    o_gz = o_gb + GDN_HEADS
    o_gates = o_gz + GDN_HEADS * GDN_DV

    c_pad = jnp.zeros((8, d), F32).at[:batch].set(c)
    xf = x.reshape(t, d)
    uq_perm = _uq_perm()
    ukv_perm = _ukv_perm()

    for l in range(depth):
        ada = _ada(c_pad, w_ada[l], b_ada[l][None, :])
        ada3 = ada[:batch].reshape(batch, N_MOD, d)

        xf = _ffn(xf, ada3, g_ffn1[l][None, :], w1_gate[l].astype(BF16), w1_up[l].astype(BF16),
                  w1_down[l].astype(BF16), g_final[None, :], mod=0, seq=seq, final_norm=False)

        wi = w_in[l]
        w_a = jnp.concatenate([wi[:, o_gq:o_ga], wi[:, o_gates:], wi[:, o_gz:o_gates]],
                              axis=1).astype(BF16)
        pe1 = wi[:, o_pe:o_pe + half]
        pe2 = wi[:, o_pe + half:o_gq]
        w_b = jnp.concatenate([wi[:, o_q:o_pe], pe1, pe1, pe2, pe2, wi[:, o_ga:o_gz],
                               jnp.zeros((d, LANE - 2 * GDN_HEADS), F32)], axis=1).astype(BF16)
        g_mix_l = g_mix[l][None, :]
        pa, lat = _inproj(xf, ada3, g_mix_l, w_a, w_b, mod=1, seq=seq)

        q, k, vt = _mla_prep(lat, g_q_lat[l][None, :], g_kv_lat[l][None, :],
                            w_uq[l][:, uq_perm].astype(BF16), w_ukv[l][:, ukv_perm].astype(BF16),
                            cos_a, sin_a, batch=batch, seq=seq)
        o_mla = _attention(q, k, vt, batch=batch, seq=seq)

        prm = jnp.zeros((8, LANE), F32)
        prm = prm.at[0, :GDN_HEADS].set(a_log[l]).at[1, :GDN_HEADS].set(dt_bias[l])
        prm = prm.at[2, :].set(g_gdn_out[l])
        o_gdn = _gdn(pa, lat, w_conv[l], prm, batch=batch, seq=seq)

        xf = _merge(o_mla, o_gdn, pa, xf, ada3, w_o_mla[l].astype(BF16), w_o_gdn[l].astype(BF16),
                    w_out[l].astype(BF16), seq=seq)

        xf = _ffn(xf, ada3, g_ffn2[l][None, :], w2_gate[l].astype(BF16), w2_up[l].astype(BF16),
                  w2_down[l].astype(BF16), g_final[None, :], mod=2, seq=seq,
                  final_norm=(l == depth - 1))
    return xf.reshape(batch, seq, d)
```
